```python
import jax, jax.numpy as jnp
from jax import lax
import numpy as np

D_MODEL = 1024
BATCH = 16
SEQ = 256
DEPTH = 4
DEC_BATCH = 8
DEC_SEQ = 1024
PAST_LEN = 512

GRID_W = 64
N_MIXERS = 2
N_RWKV_LAYERS = (DEPTH + 1) // 2
N_ATTN_LAYERS = DEPTH // 2
ALPHA = (2 * DEPTH) ** 0.25
BETA = (8 * DEPTH) ** -0.25
LN_EPS = 1e-5
RWKV_HEAD = 64
RWKV_HEADS = D_MODEL // RWKV_HEAD
DECAY_LORA = 64
ICLR_LORA = 64
GATE_LORA = 128
GN_EPS = RWKV_HEAD * 1e-5
HEAD_DIM = 64
N_HEADS = D_MODEL // HEAD_DIM
KV_HEADS = 4
GROUP = N_HEADS // KV_HEADS
Q_WIDTH = N_HEADS * HEAD_DIM
KV_WIDTH = KV_HEADS * HEAD_DIM
Q_BLOCK = 128
ROPE_THETA = 10000.0
ROPE_FREQS = HEAD_DIM // 4
ATTN_SCALE = HEAD_DIM ** -0.5
RMS_EPS = 1e-6
PEER_HEADS = 8
N_KEYS = 128
N_EXPERTS = N_KEYS * N_KEYS
PEER_QUERY = 256
PEER_HALF = PEER_QUERY // 2
PEER_TOPK = 16
PEER_BLOCK = 128

kernel_name = 'hybrid_rwkv7_gqa_peer_diffusion_step'


def residual_post_norm(x, branch, g, b):
    z = ALPHA * x.astype(jnp.float32) + branch.astype(jnp.float32)
    mu = jnp.mean(z, -1, keepdims=True)
    var = jnp.mean(jnp.square(z - mu), -1, keepdims=True)
    return ((z - mu) * lax.rsqrt(var + LN_EPS) * g + b).astype(x.dtype)


def rms_norm(x, g):
    xf = x.astype(jnp.float32)
    return (xf * lax.rsqrt(jnp.mean(xf * xf, -1, keepdims=True) + RMS_EPS) * g).astype(x.dtype)


def adaln_params(cond, w, b):
    return (jax.nn.silu(cond) @ w + b).reshape(cond.shape[0], 6, D_MODEL)


def modulate(x, shift, scale):
    return x * (1 + scale[:, None]) + shift[:, None]


def centred_shift(x):
    prev = jnp.pad(x[:, :-1], ((0, 0), (1, 0), (0, 0)))
    nxt = jnp.pad(x[:, 1:], ((0, 0), (0, 1), (0, 0)))
    return 0.5 * (prev + nxt)


def wkv_scan(r, w, k, v, kk, a, s0, reverse):
    def step(S, inp):
        r_t, w_t, k_t, v_t, kk_t, a_t = inp
        s_kk = jnp.einsum('bhij,bhj->bhi', S, kk_t)
        S = (S * w_t[:, :, None, :] - s_kk[..., None] * (kk_t * a_t)[:, :, None, :]
             + v_t[..., None] * k_t[:, :, None, :])
        return S, jnp.einsum('bhij,bhj->bhi', S, r_t)
    seq = tuple(jnp.swapaxes(t, 0, 1) for t in (r, w, k, v, kk, a))
    s_fin, y = lax.scan(step, s0, seq, reverse=reverse)
    return jnp.swapaxes(y, 0, 1), s_fin


def rwkv_time_mix(h, s0, mu, wrkv, wo, w0, w1, w2, a0, a1, a2, g1, g2, k_k, k_a, r_k, lnx_g, lnx_b):
    B, T, D = h.shape
    f32 = jnp.float32
    heads = lambda t: t.reshape(t.shape[:-1] + (RWKV_HEADS, RWKV_HEAD))
    xx = centred_shift(h) - h
    xr, xw, xk, xv, xa, xg = (h + xx * mu[m] for m in range(6))
    r = xr @ wrkv[0]
    k = xk @ wrkv[1]
    v = xv @ wrkv[2]
    wl = w0[:, None, None, :] + jnp.einsum('zbtl,zld->zbtd', jnp.tanh(jnp.einsum('btd,zdl->zbtl', xw, w1)), w2)
    decay = jnp.exp(-jnp.exp(-jax.nn.softplus(-wl.astype(f32)) - 0.5))
    a = jax.nn.sigmoid((a0[:, None, None, :] + jnp.einsum('zbtl,zld->zbtd', jnp.einsum('btd,zdl->zbtl', xa, a1), a2)).astype(f32))
    g = jax.nn.sigmoid(xg @ g1) @ g2
    kk = heads((k * k_k).astype(f32))
    kk = kk / jnp.maximum(jnp.sqrt(jnp.sum(kk * kk, -1, keepdims=True)), 1e-12)
    kd = k.astype(f32)[None] * (1 + (a - 1) * k_a.astype(f32))
    rf, vf = heads(r.astype(f32)), heads(v.astype(f32))
    s0 = s0.astype(f32)
    y_f, s_f = wkv_scan(rf, heads(decay[0]), heads(kd[0]), vf, kk, heads(a[0]), s0[:, 0], False)
    y_b, s_b = wkv_scan(rf, heads(decay[1]), heads(kd[1]), vf, kk, heads(a[1]), s0[:, 1], True)
    y = y_f + y_b
    ym = jnp.mean(y, -1, keepdims=True)
    yv = jnp.mean(jnp.square(y - ym), -1, keepdims=True)
    yn = ((y - ym) * lax.rsqrt(yv + GN_EPS)).reshape(B, T, D) * lnx_g + lnx_b
    bonus = jnp.sum(rf[None] * heads(kd) * r_k, axis=(0, -1))[..., None] * vf
    out = ((yn + bonus.reshape(B, T, D)).astype(h.dtype) * g) @ wo
    return out, jnp.stack([s_f, s_b], axis=1)


def axial_rope_angles(n):
    rows = n // GRID_W
    row = jnp.repeat(jnp.arange(rows), GRID_W).astype(jnp.float32)
    col = (jnp.arange(rows * GRID_W) % GRID_W).astype(jnp.float32)
    freqs = ROPE_THETA ** (-jnp.arange(ROPE_FREQS, dtype=jnp.float32) / ROPE_FREQS)
    ang = jnp.stack([row[:, None] * freqs, col[:, None] * freqs], axis=1)
    return jnp.cos(ang), jnp.sin(ang)


def apply_axial_rope(x, cos, sin):
    xs = x.astype(jnp.float32).reshape(x.shape[:-1] + (2, 2, ROPE_FREQS))
    bshape = (x.shape[1],) + (1,) * (x.ndim - 3) + (2, ROPE_FREQS)
    cos, sin = cos.reshape(bshape), sin.reshape(bshape)
    x1, x2 = xs[..., 0, :], xs[..., 1, :]
    out = jnp.stack([x1 * cos - x2 * sin, x2 * cos + x1 * sin], axis=-2)
    return out.reshape(x.shape).astype(x.dtype)


def attn_project(h, wqkv, qn, kn):
    B, n, _ = h.shape
    qkv = h @ wqkv
    q = rms_norm(qkv[..., :Q_WIDTH].reshape(B, n, KV_HEADS, GROUP, HEAD_DIM), qn)
    k = rms_norm(qkv[..., Q_WIDTH:Q_WIDTH + KV_WIDTH].reshape(B, n, KV_HEADS, HEAD_DIM), kn)
    v = qkv[..., Q_WIDTH + KV_WIDTH:].reshape(B, n, KV_HEADS, HEAD_DIM)
    return q, k, v


def block_attention(q, k, v):
    B, T = q.shape[:2]
    qb = jnp.moveaxis(q.reshape((B, T // Q_BLOCK, Q_BLOCK) + q.shape[2:]), 1, 0)
    def one_block(qi):
        s = jnp.einsum('bqkgd,bskd->bkgqs', qi, k).astype(jnp.float32) * ATTN_SCALE
        p = jax.nn.softmax(s, axis=-1).astype(v.dtype)
        return jnp.einsum('bkgqs,bskd->bqkgd', p, v)
    o = lax.map(one_block, qb)
    return jnp.moveaxis(o, 0, 1).reshape(q.shape)


def attention_context(h, wqkv, wo, qn, kn):
    q, k, v = attn_project(h, wqkv, qn, kn)
    o = block_attention(q, k, v)
    return o.reshape(h.shape[0], h.shape[1], Q_WIDTH) @ wo, k, v


def attention_latent(h, ctx_k, ctx_v, wqkv, wo, qn, kn):
    q, k, v = attn_project(h, wqkv, qn, kn)
    cos, sin = axial_rope_angles(h.shape[1])
    q = apply_axial_rope(q, cos, sin)
    k = apply_axial_rope(k, cos, sin)
    keys = jnp.concatenate([ctx_k.astype(k.dtype), k], axis=1)
    vals = jnp.concatenate([ctx_v.astype(v.dtype), v], axis=1)
    o = block_attention(q, keys, vals)
    return o.reshape(h.shape[0], h.shape[1], Q_WIDTH) @ wo


def peer_ffn(h, wq, sub_keys, u, v):
    B, n, D = h.shape
    x = h.reshape(B * n, D)
    q = (x @ wq).reshape(B * n, PEER_HEADS, 2, PEER_HALF)
    s = jnp.einsum('thzd,zkd->thzk', q, sub_keys).astype(jnp.float32)
    sv, si = lax.top_k(s, PEER_TOPK)
    cand = (sv[:, :, 0, :, None] + sv[:, :, 1, None, :]).reshape(B * n, PEER_HEADS, PEER_TOPK * PEER_TOPK)
    cv, ci = lax.top_k(cand, PEER_TOPK)
    i1 = jnp.take_along_axis(si[:, :, 0], ci // PEER_TOPK, axis=-1)
    i2 = jnp.take_along_axis(si[:, :, 1], ci % PEER_TOPK, axis=-1)
    idx = (i1 * N_KEYS + i2).reshape(-1, PEER_BLOCK, PEER_HEADS * PEER_TOPK)
    gate = jax.nn.softmax(cv, axis=-1).astype(x.dtype).reshape(-1, PEER_BLOCK, PEER_HEADS * PEER_TOPK)
    xb = x.reshape(-1, PEER_BLOCK, D)
    def experts(args):
        xt, it, gt = args
        act = jax.nn.gelu(jnp.einsum('tkd,td->tk', u[it], xt), approximate=False)
        return jnp.einsum('tk,tkd->td', gt * act, v[it])
    out = lax.map(experts, (xb, idx, gate))
    return out.reshape(B, n, D)


def setup_inputs(seed: int = 0) -> dict:
    key = jax.random.key(seed)
    ks = iter(jax.random.split(key, 40))
    nrm = lambda shape, scale: jax.random.normal(next(ks), shape, jnp.float32) * scale
    D = D_MODEL
    NR, NA = N_RWKV_LAYERS, N_ATTN_LAYERS
    return {
        'x_prompt': nrm((BATCH, SEQ, D), 1.0),
        'x_sample': nrm((DEC_BATCH, DEC_SEQ, D), 1.0),
        'c': nrm((DEC_BATCH, D), 1.0),
        'state_rwkv': nrm((DEC_BATCH, NR, 2, RWKV_HEADS, RWKV_HEAD, RWKV_HEAD), 1.0),
        'cache_k': nrm((DEC_BATCH, NA, PAST_LEN, KV_HEADS, HEAD_DIM), 1.0),
        'cache_v': nrm((DEC_BATCH, NA, PAST_LEN, KV_HEADS, HEAD_DIM), 1.0),
        'c_ctx': nrm((D,), 1.0),
        'ada_w': nrm((DEPTH, D, 6 * D), 0.5 * D ** -0.5),
        'ada_b': nrm((DEPTH, 6 * D), 0.02),
        'ln_g': 1.0 + nrm((DEPTH, 2, D), 0.02),
        'ln_b': nrm((DEPTH, 2, D), 0.02),
        'rwkv_mu': jax.random.uniform(next(ks), (NR, 6, D), jnp.float32),
        'rwkv_wrkv': nrm((NR, 3, D, D), D ** -0.5),
        'rwkv_wo': nrm((NR, D, D), BETA * D ** -0.5),
        'rwkv_w0': jax.random.uniform(next(ks), (NR, 2, D), jnp.float32, -6.0, -1.0),
        'rwkv_w1': nrm((NR, 2, D, DECAY_LORA), 0.1 * D ** -0.5),
        'rwkv_w2': nrm((NR, 2, DECAY_LORA, D), 0.1 * DECAY_LORA ** -0.5),
        'rwkv_a0': nrm((NR, 2, D), 0.1),
        'rwkv_a1': nrm((NR, 2, D, ICLR_LORA), 0.1 * D ** -0.5),
        'rwkv_a2': nrm((NR, 2, ICLR_LORA, D), 0.1 * ICLR_LORA ** -0.5),
        'rwkv_g1': nrm((NR, D, GATE_LORA), D ** -0.5),
        'rwkv_g2': nrm((NR, GATE_LORA, D), GATE_LORA ** -0.5),
        'rwkv_kk': 0.85 + nrm((NR, D), 0.05),
        'rwkv_ka': 1.0 + nrm((NR, D), 0.05),
        'rwkv_rk': nrm((NR, RWKV_HEADS, RWKV_HEAD), 0.1),
        'rwkv_lnx_g': 1.0 + nrm((NR, D), 0.02),
        'rwkv_lnx_b': nrm((NR, D), 0.02),
        'attn_wqkv': nrm((NA, D, Q_WIDTH + 2 * KV_WIDTH), D ** -0.5),
        'attn_wo': nrm((NA, Q_WIDTH, D), BETA * Q_WIDTH ** -0.5),
        'attn_qn': 1.0 + nrm((NA, HEAD_DIM), 0.02),
        'attn_kn': 1.0 + nrm((NA, HEAD_DIM), 0.02),
        'peer_wq': nrm((DEPTH, D, PEER_HEADS * PEER_QUERY), D ** -0.5),
        'peer_keys': nrm((DEPTH, 2, N_KEYS, PEER_HALF), PEER_HALF ** -0.5),
        'peer_u': nrm((DEPTH, N_EXPERTS, D), D ** -0.5),
        'peer_v': nrm((DEPTH, N_EXPERTS, D), BETA),
    }


def reference(x_prompt, x_sample, c, state_rwkv, cache_k, cache_v, c_ctx, ada_w, ada_b, ln_g, ln_b,
              rwkv_mu, rwkv_wrkv, rwkv_wo, rwkv_w0, rwkv_w1, rwkv_w2, rwkv_a0, rwkv_a1, rwkv_a2,
              rwkv_g1, rwkv_g2, rwkv_kk, rwkv_ka, rwkv_rk, rwkv_lnx_g, rwkv_lnx_b,
              attn_wqkv, attn_wo, attn_qn, attn_kn, peer_wq, peer_keys, peer_u, peer_v):
    xp, xs = x_prompt, x_sample
    states, keys_out, vals_out = [], [], []
    for i in range(DEPTH):
        j = i // N_MIXERS
        mod_p = adaln_params(c_ctx[None], ada_w[i], ada_b[i])
        mod_s = adaln_params(c, ada_w[i], ada_b[i])
        hp = modulate(xp, mod_p[:, 0], mod_p[:, 1])
        hs = modulate(xs, mod_s[:, 0], mod_s[:, 1])
        if i % N_MIXERS == 0:
            rw = (rwkv_mu[j], rwkv_wrkv[j], rwkv_wo[j], rwkv_w0[j], rwkv_w1[j], rwkv_w2[j],
                  rwkv_a0[j], rwkv_a1[j], rwkv_a2[j], rwkv_g1[j], rwkv_g2[j], rwkv_kk[j], rwkv_ka[j],
                  rwkv_rk[j], rwkv_lnx_g[j], rwkv_lnx_b[j])
            s_zero = jnp.zeros((xp.shape[0], 2, RWKV_HEADS, RWKV_HEAD, RWKV_HEAD), jnp.float32)
            op, s_ctx = rwkv_time_mix(hp, s_zero, *rw)
            os_, _ = rwkv_time_mix(hs, state_rwkv[:, j], *rw)
            states.append(s_ctx)
        else:
            op, kp, vp = attention_context(hp, attn_wqkv[j], attn_wo[j], attn_qn[j], attn_kn[j])
            os_ = attention_latent(hs, cache_k[:, j], cache_v[:, j], attn_wqkv[j], attn_wo[j], attn_qn[j], attn_kn[j])
            keys_out.append(kp)
            vals_out.append(vp)
        xp = residual_post_norm(xp, mod_p[:, 2][:, None] * op, ln_g[i, 0], ln_b[i, 0])
        xs = residual_post_norm(xs, mod_s[:, 2][:, None] * os_, ln_g[i, 0], ln_b[i, 0])
        hp = modulate(xp, mod_p[:, 3], mod_p[:, 4])
        hs = modulate(xs, mod_s[:, 3], mod_s[:, 4])
        fp = peer_ffn(hp, peer_wq[i], peer_keys[i], peer_u[i], peer_v[i])
        fs = peer_ffn(hs, peer_wq[i], peer_keys[i], peer_u[i], peer_v[i])
        xp = residual_post_norm(xp, mod_p[:, 5][:, None] * fp, ln_g[i, 1], ln_b[i, 1])
        xs = residual_post_norm(xs, mod_s[:, 5][:, None] * fs, ln_g[i, 1], ln_b[i, 1])
    new_state_rwkv = jnp.stack(states, axis=1)
    new_cache_k = jnp.stack(keys_out, axis=1)
    new_cache_v = jnp.stack(vals_out, axis=1)
    return (xp, xs, new_state_rwkv, new_cache_k, new_cache_v)
```

```python
import functools

import jax
import jax.numpy as jnp
from jax import lax
from jax.experimental import pallas as pl
from jax.experimental.pallas import tpu as pltpu

D_MODEL = 1024
DEPTH = 4
GRID_W = 64
ALPHA = (2 * DEPTH) ** 0.25
LN_EPS = 1e-5
RWKV_HEAD = 64
RWKV_HEADS = D_MODEL // RWKV_HEAD
GN_EPS = RWKV_HEAD * 1e-5
HEAD_DIM = 64
N_HEADS = D_MODEL // HEAD_DIM
KV_HEADS = 4
GROUP = N_HEADS // KV_HEADS
Q_WIDTH = N_HEADS * HEAD_DIM
KV_WIDTH = KV_HEADS * HEAD_DIM
Q_BLOCK = 128
ROPE_THETA = 10000.0
ROPE_FREQS = HEAD_DIM // 4
ATTN_SCALE = HEAD_DIM ** -0.5
RMS_EPS = 1e-6
PEER_HEADS = 8
N_KEYS = 128
PEER_QUERY = 256
PEER_HALF = PEER_QUERY // 2
PEER_TOPK = 16
PEER_BLOCK = 128


def _mm_kernel(x_ref, w_ref, o_ref):
    o_ref[...] = jnp.dot(x_ref[...].astype(jnp.bfloat16), w_ref[...].astype(jnp.bfloat16),
                         preferred_element_type=jnp.float32)


def matmul(x, w, tm=512, tn=512):
    M, K = x.shape
    N = w.shape[1]
    tm = min(tm, M)
    tn = min(tn, N)
    assert M % tm == 0 and N % tn == 0
    return pl.pallas_call(
        _mm_kernel,
        grid=(M // tm, N // tn),
        in_specs=[pl.BlockSpec((tm, K), lambda i, j: (i, 0)),
                  pl.BlockSpec((K, tn), lambda i, j: (0, j))],
        out_specs=pl.BlockSpec((tm, tn), lambda i, j: (i, j)),
        out_shape=jax.ShapeDtypeStruct((M, N), jnp.float32),
        compiler_params=pltpu.CompilerParams(dimension_semantics=("parallel", "parallel")),
        name="matmul",
    )(x, w)


def residual_post_norm(x, branch, g, b):
    z = ALPHA * x + branch
    mu = jnp.mean(z, -1, keepdims=True)
    var = jnp.mean(jnp.square(z - mu), -1, keepdims=True)
    return (z - mu) * lax.rsqrt(var + LN_EPS) * g + b


def rms_norm(x, g):
    return x * lax.rsqrt(jnp.mean(x * x, -1, keepdims=True) + RMS_EPS) * g


def adaln_params(cond, w, b):
    return (jax.nn.silu(cond) @ w + b).reshape(cond.shape[0], 6, D_MODEL)


def modulate(x, shift, scale):
    return x * (1 + scale[:, None]) + shift[:, None]


def centred_shift(x):
    prev = jnp.pad(x[:, :-1], ((0, 0), (1, 0), (0, 0)))
    nxt = jnp.pad(x[:, 1:], ((0, 0), (0, 1), (0, 0)))
    return 0.5 * (prev + nxt)


def wkv_scan(r, w, k, v, kk, a, s0, reverse):
    def step(S, inp):
        r_t, w_t, k_t, v_t, kk_t, a_t = inp
        s_kk = jnp.einsum('bhij,bhj->bhi', S, kk_t)
        S = (S * w_t[:, :, None, :] - s_kk[..., None] * (kk_t * a_t)[:, :, None, :]
             + v_t[..., None] * k_t[:, :, None, :])
        return S, jnp.einsum('bhij,bhj->bhi', S, r_t)
    seq = tuple(jnp.swapaxes(t, 0, 1) for t in (r, w, k, v, kk, a))
    s_fin, y = lax.scan(step, s0, seq, reverse=reverse)
    return jnp.swapaxes(y, 0, 1), s_fin


def rwkv_time_mix(h, s0, mu, wrkv, wo, w0, w1, w2, a0, a1, a2, g1, g2, k_k, k_a, r_k, lnx_g, lnx_b):
    B, T, D = h.shape
    f32 = jnp.float32
    heads = lambda t: t.reshape(t.shape[:-1] + (RWKV_HEADS, RWKV_HEAD))
    xx = centred_shift(h) - h
    xr, xw, xk, xv, xa, xg = (h + xx * mu[m] for m in range(6))
    r = xr @ wrkv[0]
    k = xk @ wrkv[1]
    v = xv @ wrkv[2]
    wl = w0[:, None, None, :] + jnp.einsum('zbtl,zld->zbtd', jnp.tanh(jnp.einsum('btd,zdl->zbtl', xw, w1)), w2)
    decay = jnp.exp(-jnp.exp(-jax.nn.softplus(-wl.astype(f32)) - 0.5))
    a = jax.nn.sigmoid((a0[:, None, None, :] + jnp.einsum('zbtl,zld->zbtd', jnp.einsum('btd,zdl->zbtl', xa, a1), a2)).astype(f32))
    g = jax.nn.sigmoid(xg @ g1) @ g2
    kk = heads((k * k_k).astype(f32))
    kk = kk / jnp.maximum(jnp.sqrt(jnp.sum(kk * kk, -1, keepdims=True)), 1e-12)
    kd = k.astype(f32)[None] * (1 + (a - 1) * k_a.astype(f32))
    rf, vf = heads(r.astype(f32)), heads(v.astype(f32))
    s0 = s0.astype(f32)
    y_f, s_f = wkv_scan(rf, heads(decay[0]), heads(kd[0]), vf, kk, heads(a[0]), s0[:, 0], False)
    y_b, s_b = wkv_scan(rf, heads(decay[1]), heads(kd[1]), vf, kk, heads(a[1]), s0[:, 1], True)
    y = y_f + y_b
    ym = jnp.mean(y, -1, keepdims=True)
    yv = jnp.mean(jnp.square(y - ym), -1, keepdims=True)
    yn = ((y - ym) * lax.rsqrt(yv + GN_EPS)).reshape(B, T, D) * lnx_g + lnx_b
    bonus = jnp.sum(rf[None] * heads(kd) * r_k, axis=(0, -1))[..., None] * vf
    out = ((yn + bonus.reshape(B, T, D)).astype(h.dtype) * g) @ wo
    return out, jnp.stack([s_f, s_b], axis=1)


def axial_rope_angles(n):
    rows = n // GRID_W
    row = jnp.repeat(jnp.arange(rows), GRID_W).astype(jnp.float32)
    col = (jnp.arange(rows * GRID_W) % GRID_W).astype(jnp.float32)
    freqs = ROPE_THETA ** (-jnp.arange(ROPE_FREQS, dtype=jnp.float32) / ROPE_FREQS)
    ang = jnp.stack([row[:, None] * freqs, col[:, None] * freqs], axis=1)
    return jnp.cos(ang), jnp.sin(ang)


def apply_axial_rope(x, cos, sin):
    xs = x.astype(jnp.float32).reshape(x.shape[:-1] + (2, 2, ROPE_FREQS))
    bshape = (x.shape[1],) + (1,) * (x.ndim - 3) + (2, ROPE_FREQS)
    cos, sin = cos.reshape(bshape), sin.reshape(bshape)
    x1, x2 = xs[..., 0, :], xs[..., 1, :]
    out = jnp.stack([x1 * cos - x2 * sin, x2 * cos + x1 * sin], axis=-2)
    return out.reshape(x.shape).astype(x.dtype)


def attn_project(h, wqkv, qn, kn):
    B, n, _ = h.shape
    qkv = h @ wqkv
    q = rms_norm(qkv[..., :Q_WIDTH].reshape(B, n, KV_HEADS, GROUP, HEAD_DIM), qn)
    k = rms_norm(qkv[..., Q_WIDTH:Q_WIDTH + KV_WIDTH].reshape(B, n, KV_HEADS, HEAD_DIM), kn)
    v = qkv[..., Q_WIDTH + KV_WIDTH:].reshape(B, n, KV_HEADS, HEAD_DIM)
    return q, k, v


def block_attention(q, k, v):
    B, T = q.shape[:2]
    qb = jnp.moveaxis(q.reshape((B, T // Q_BLOCK, Q_BLOCK) + q.shape[2:]), 1, 0)
    def one_block(qi):
        s = jnp.einsum('bqkgd,bskd->bkgqs', qi, k).astype(jnp.float32) * ATTN_SCALE
        p = jax.nn.softmax(s, axis=-1).astype(v.dtype)
        return jnp.einsum('bkgqs,bskd->bqkgd', p, v)
    o = lax.map(one_block, qb)
    return jnp.moveaxis(o, 0, 1).reshape(q.shape)


def attention_context(h, wqkv, wo, qn, kn):
    q, k, v = attn_project(h, wqkv, qn, kn)
    o = block_attention(q, k, v)
    return o.reshape(h.shape[0], h.shape[1], Q_WIDTH) @ wo, k, v


def attention_latent(h, ctx_k, ctx_v, wqkv, wo, qn, kn):
    q, k, v = attn_project(h, wqkv, qn, kn)
    cos, sin = axial_rope_angles(h.shape[1])
    q = apply_axial_rope(q, cos, sin)
    k = apply_axial_rope(k, cos, sin)
    keys = jnp.concatenate([ctx_k.astype(k.dtype), k], axis=1)
    vals = jnp.concatenate([ctx_v.astype(v.dtype), v], axis=1)
    o = block_attention(q, keys, vals)
    return o.reshape(h.shape[0], h.shape[1], Q_WIDTH) @ wo


def peer_ffn(h, wq, sub_keys, u, v):
    B, n, D = h.shape
    x = h.reshape(B * n, D)
    q = matmul(x, wq).reshape(B * n, PEER_HEADS, 2, PEER_HALF)
    s = jnp.einsum('thzd,zkd->thzk', q, sub_keys).astype(jnp.float32)
    sv, si = lax.top_k(s, PEER_TOPK)
    cand = (sv[:, :, 0, :, None] + sv[:, :, 1, None, :]).reshape(B * n, PEER_HEADS, PEER_TOPK * PEER_TOPK)
    cv, ci = lax.top_k(cand, PEER_TOPK)
    i1 = jnp.take_along_axis(si[:, :, 0], ci // PEER_TOPK, axis=-1)
    i2 = jnp.take_along_axis(si[:, :, 1], ci % PEER_TOPK, axis=-1)
    idx = (i1 * N_KEYS + i2).reshape(-1, PEER_BLOCK, PEER_HEADS * PEER_TOPK)
    gate = jax.nn.softmax(cv, axis=-1).astype(x.dtype).reshape(-1, PEER_BLOCK, PEER_HEADS * PEER_TOPK)
    xb = x.reshape(-1, PEER_BLOCK, D)
    def experts(args):
        xt, it, gt = args
        act = jax.nn.gelu(jnp.einsum('tkd,td->tk', u[it], xt), approximate=False)
        return jnp.einsum('tk,tkd->td', gt * act, v[it])
    out = lax.map(experts, (xb, idx, gate))
    return out.reshape(B, n, D)


def kernel(x_prompt, x_sample, c, state_rwkv, cache_k, cache_v, c_ctx, ada_w, ada_b, ln_g, ln_b,
           rwkv_mu, rwkv_wrkv, rwkv_wo, rwkv_w0, rwkv_w1, rwkv_w2, rwkv_a0, rwkv_a1, rwkv_a2,
           rwkv_g1, rwkv_g2, rwkv_kk, rwkv_ka, rwkv_rk, rwkv_lnx_g, rwkv_lnx_b,
           attn_wqkv, attn_wo, attn_qn, attn_kn, peer_wq, peer_keys, peer_u, peer_v):
    xp, xs = x_prompt, x_sample
    states, keys_out, vals_out = [], [], []
    for i in range(DEPTH):
        j = i // 2
        mod_p = adaln_params(c_ctx[None], ada_w[i], ada_b[i])
        mod_s = adaln_params(c, ada_w[i], ada_b[i])
        hp = modulate(xp, mod_p[:, 0], mod_p[:, 1])
        hs = modulate(xs, mod_s[:, 0], mod_s[:, 1])
        if i % 2 == 0:
            rw = (rwkv_mu[j], rwkv_wrkv[j], rwkv_wo[j], rwkv_w0[j], rwkv_w1[j], rwkv_w2[j],
                  rwkv_a0[j], rwkv_a1[j], rwkv_a2[j], rwkv_g1[j], rwkv_g2[j], rwkv_kk[j], rwkv_ka[j],
                  rwkv_rk[j], rwkv_lnx_g[j], rwkv_lnx_b[j])
            s_zero = jnp.zeros((xp.shape[0], 2, RWKV_HEADS, RWKV_HEAD, RWKV_HEAD), jnp.float32)
            op, s_ctx = rwkv_time_mix(hp, s_zero, *rw)
            os_, _ = rwkv_time_mix(hs, state_rwkv[:, j], *rw)
            states.append(s_ctx)
        else:
            op, kp, vp = attention_context(hp, attn_wqkv[j], attn_wo[j], attn_qn[j], attn_kn[j])
            os_ = attention_latent(hs, cache_k[:, j], cache_v[:, j], attn_wqkv[j], attn_wo[j], attn_qn[j], attn_kn[j])
            keys_out.append(kp)
            vals_out.append(vp)
        xp = residual_post_norm(xp, mod_p[:, 2][:, None] * op, ln_g[i, 0], ln_b[i, 0])
        xs = residual_post_norm(xs, mod_s[:, 2][:, None] * os_, ln_g[i, 0], ln_b[i, 0])
        hp = modulate(xp, mod_p[:, 3], mod_p[:, 4])
        hs = modulate(xs, mod_s[:, 3], mod_s[:, 4])
        fp = peer_ffn(hp, peer_wq[i], peer_keys[i], peer_u[i], peer_v[i])
        fs = peer_ffn(hs, peer_wq[i], peer_keys[i], peer_u[i], peer_v[i])
        xp = residual_post_norm(xp, mod_p[:, 5][:, None] * fp, ln_g[i, 1], ln_b[i, 1])
        xs = residual_post_norm(xs, mod_s[:, 5][:, None] * fs, ln_g[i, 1], ln_b[i, 1])
    return (xp, xs, jnp.stack(states, axis=1), jnp.stack(keys_out, axis=1), jnp.stack(vals_out, axis=1))
```

```python
import functools

import jax
import jax.numpy as jnp
from jax import lax
from jax.experimental import pallas as pl
from jax.experimental.pallas import tpu as pltpu

D_MODEL = 1024
DEPTH = 4
GRID_W = 64
ALPHA = (2 * DEPTH) ** 0.25
LN_EPS = 1e-5
RWKV_HEAD = 64
RWKV_HEADS = D_MODEL // RWKV_HEAD
GN_EPS = RWKV_HEAD * 1e-5
HEAD_DIM = 64
N_HEADS = D_MODEL // HEAD_DIM
KV_HEADS = 4
GROUP = N_HEADS // KV_HEADS
Q_WIDTH = N_HEADS * HEAD_DIM
KV_WIDTH = KV_HEADS * HEAD_DIM
Q_BLOCK = 128
ROPE_THETA = 10000.0
ROPE_FREQS = HEAD_DIM // 4
ATTN_SCALE = HEAD_DIM ** -0.5
RMS_EPS = 1e-6
PEER_HEADS = 8
N_KEYS = 128
PEER_QUERY = 256
PEER_HALF = PEER_QUERY // 2
PEER_TOPK = 16
PEER_BLOCK = 128


def _mm_kernel(x_ref, w_ref, o_ref):
    o_ref[...] = jnp.dot(x_ref[...].astype(jnp.bfloat16), w_ref[...].astype(jnp.bfloat16),
                         preferred_element_type=jnp.float32)


def matmul(x, w, tm=512, tn=512):
    M, K = x.shape
    N = w.shape[1]
    tm = min(tm, M)
    tn = min(tn, N)
    assert M % tm == 0 and N % tn == 0
    return pl.pallas_call(
        _mm_kernel,
        grid=(M // tm, N // tn),
        in_specs=[pl.BlockSpec((tm, K), lambda i, j: (i, 0)),
                  pl.BlockSpec((K, tn), lambda i, j: (0, j))],
        out_specs=pl.BlockSpec((tm, tn), lambda i, j: (i, j)),
        out_shape=jax.ShapeDtypeStruct((M, N), jnp.float32),
        compiler_params=pltpu.CompilerParams(dimension_semantics=("parallel", "parallel")),
        name="matmul",
    )(x, w)


def residual_post_norm(x, branch, g, b):
    z = ALPHA * x + branch
    mu = jnp.mean(z, -1, keepdims=True)
    var = jnp.mean(jnp.square(z - mu), -1, keepdims=True)
    return (z - mu) * lax.rsqrt(var + LN_EPS) * g + b


def rms_norm(x, g):
    return x * lax.rsqrt(jnp.mean(x * x, -1, keepdims=True) + RMS_EPS) * g


def adaln_params(cond, w, b):
    return (jax.nn.silu(cond) @ w + b).reshape(cond.shape[0], 6, D_MODEL)


def modulate(x, shift, scale):
    return x * (1 + scale[:, None]) + shift[:, None]


def centred_shift(x):
    prev = jnp.pad(x[:, :-1], ((0, 0), (1, 0), (0, 0)))
    nxt = jnp.pad(x[:, 1:], ((0, 0), (0, 1), (0, 0)))
    return 0.5 * (prev + nxt)


LANES = 128


def _wkv_scan_kernel(r_ref, w_ref, k_ref, v_ref, kk_ref, ka_ref, s0_ref, y_ref, sf_ref, st_ref, *, tc):
    d = pl.program_id(0)
    c = pl.program_id(2)
    n = RWKV_HEAD

    @pl.when(c == 0)
    def _init():
        st_ref[...] = s0_ref[...]

    def step(tt, carry):
        t = jnp.where(d == 0, tt, tc - 1 - tt)
        skk = jnp.zeros((n, LANES), jnp.float32)
        for j in range(n):
            skk = skk + st_ref[j] * kk_ref[t, pl.ds(j, 1), :]
        v_t = v_ref[t]
        y = jnp.zeros((n, LANES), jnp.float32)
        for j in range(n):
            s = (st_ref[j] * w_ref[t, pl.ds(j, 1), :] - skk * ka_ref[t, pl.ds(j, 1), :]
                 + v_t * k_ref[t, pl.ds(j, 1), :])
            st_ref[j] = s
            y = y + s * r_ref[t, pl.ds(j, 1), :]
        y_ref[t] = y
        return carry

    lax.fori_loop(0, tc, step, 0)

    @pl.when(c == pl.num_programs(2) - 1)
    def _fin():
        sf_ref[...] = st_ref[...]


def wkv_scan_lanes(r, w, k, v, kk, ka, s0, *, tc=16):
    T, n, L = r.shape
    assert n == RWKV_HEAD and L % LANES == 0 and T % tc == 0
    nc = T // tc
    tmap = lambda d, g, c: (jnp.where(d == 0, c, nc - 1 - c), 0, g)
    dmap = lambda d, g, c: (d, jnp.where(d == 0, c, nc - 1 - c), 0, g)
    shared = pl.BlockSpec((tc, n, LANES), tmap)
    perdir = pl.BlockSpec((None, tc, n, LANES), dmap)
    state = pl.BlockSpec((None, n, n, LANES), lambda d, g, c: (d, 0, 0, g))
    return pl.pallas_call(
        functools.partial(_wkv_scan_kernel, tc=tc),
        grid=(2, L // LANES, nc),
        in_specs=[shared, perdir, perdir, shared, shared, perdir, state],
        out_specs=[perdir, state],
        out_shape=[jax.ShapeDtypeStruct((2, T, n, L), jnp.float32),
                   jax.ShapeDtypeStruct((2, n, n, L), jnp.float32)],
        scratch_shapes=[pltpu.VMEM((n, n, LANES), jnp.float32)],
        compiler_params=pltpu.CompilerParams(dimension_semantics=("parallel", "parallel", "arbitrary")),
        name="wkv_scan",
    )(r, w, k, v, kk, ka, s0)


def _to_lanes(x):
    B, T, H, N = x.shape
    return x.transpose(1, 3, 0, 2).reshape(T, N, B * H)


def wkv_bidir(r, w, k, v, kk, a, s0):
    B, T, H, N = r.shape
    both = lambda x: jnp.stack([_to_lanes(x[0]), _to_lanes(x[1])])
    s0l = s0.transpose(1, 4, 3, 0, 2).reshape(2, N, N, B * H)
    y, sf = wkv_scan_lanes(_to_lanes(r), both(w), both(k), _to_lanes(v), _to_lanes(kk), both(kk[None] * a), s0l)
    y = (y[0] + y[1]).reshape(T, N, B, H).transpose(2, 0, 3, 1)
    sf = sf.reshape(2, N, N, B, H).transpose(3, 0, 4, 2, 1)
    return y, sf


def rwkv_time_mix(h, s0, mu, wrkv, wo, w0, w1, w2, a0, a1, a2, g1, g2, k_k, k_a, r_k, lnx_g, lnx_b):
    B, T, D = h.shape
    f32 = jnp.float32
    heads = lambda t: t.reshape(t.shape[:-1] + (RWKV_HEADS, RWKV_HEAD))
    xx = centred_shift(h) - h
    xr, xw, xk, xv, xa, xg = (h + xx * mu[m] for m in range(6))
    r = xr @ wrkv[0]
    k = xk @ wrkv[1]
    v = xv @ wrkv[2]
    wl = w0[:, None, None, :] + jnp.einsum('zbtl,zld->zbtd', jnp.tanh(jnp.einsum('btd,zdl->zbtl', xw, w1)), w2)
    decay = jnp.exp(-jnp.exp(-jax.nn.softplus(-wl.astype(f32)) - 0.5))
    a = jax.nn.sigmoid((a0[:, None, None, :] + jnp.einsum('zbtl,zld->zbtd', jnp.einsum('btd,zdl->zbtl', xa, a1), a2)).astype(f32))
    g = jax.nn.sigmoid(xg @ g1) @ g2
    kk = heads((k * k_k).astype(f32))
    kk = kk / jnp.maximum(jnp.sqrt(jnp.sum(kk * kk, -1, keepdims=True)), 1e-12)
    kd = k.astype(f32)[None] * (1 + (a - 1) * k_a.astype(f32))
    rf, vf = heads(r.astype(f32)), heads(v.astype(f32))
    s0 = s0.astype(f32)
    y, s_fin = wkv_bidir(rf, heads(decay), heads(kd), vf, kk, heads(a), s0)
    ym = jnp.mean(y, -1, keepdims=True)
    yv = jnp.mean(jnp.square(y - ym), -1, keepdims=True)
    yn = ((y - ym) * lax.rsqrt(yv + GN_EPS)).reshape(B, T, D) * lnx_g + lnx_b
    bonus = jnp.sum(rf[None] * heads(kd) * r_k, axis=(0, -1))[..., None] * vf
    out = ((yn + bonus.reshape(B, T, D)).astype(h.dtype) * g) @ wo
    return out, s_fin


def axial_rope_angles(n):
    rows = n // GRID_W
    row = jnp.repeat(jnp.arange(rows), GRID_W).astype(jnp.float32)
    col = (jnp.arange(rows * GRID_W) % GRID_W).astype(jnp.float32)
    freqs = ROPE_THETA ** (-jnp.arange(ROPE_FREQS, dtype=jnp.float32) / ROPE_FREQS)
    ang = jnp.stack([row[:, None] * freqs, col[:, None] * freqs], axis=1)
    return jnp.cos(ang), jnp.sin(ang)


def apply_axial_rope(x, cos, sin):
    xs = x.astype(jnp.float32).reshape(x.shape[:-1] + (2, 2, ROPE_FREQS))
    bshape = (x.shape[1],) + (1,) * (x.ndim - 3) + (2, ROPE_FREQS)
    cos, sin = cos.reshape(bshape), sin.reshape(bshape)
    x1, x2 = xs[..., 0, :], xs[..., 1, :]
    out = jnp.stack([x1 * cos - x2 * sin, x2 * cos + x1 * sin], axis=-2)
    return out.reshape(x.shape).astype(x.dtype)


def attn_project(h, wqkv, qn, kn):
    B, n, _ = h.shape
    qkv = h @ wqkv
    q = rms_norm(qkv[..., :Q_WIDTH].reshape(B, n, KV_HEADS, GROUP, HEAD_DIM), qn)
    k = rms_norm(qkv[..., Q_WIDTH:Q_WIDTH + KV_WIDTH].reshape(B, n, KV_HEADS, HEAD_DIM), kn)
    v = qkv[..., Q_WIDTH + KV_WIDTH:].reshape(B, n, KV_HEADS, HEAD_DIM)
    return q, k, v


def block_attention(q, k, v):
    B, T = q.shape[:2]
    qb = jnp.moveaxis(q.reshape((B, T // Q_BLOCK, Q_BLOCK) + q.shape[2:]), 1, 0)
    def one_block(qi):
        s = jnp.einsum('bqkgd,bskd->bkgqs', qi, k).astype(jnp.float32) * ATTN_SCALE
        p = jax.nn.softmax(s, axis=-1).astype(v.dtype)
        return jnp.einsum('bkgqs,bskd->bqkgd', p, v)
    o = lax.map(one_block, qb)
    return jnp.moveaxis(o, 0, 1).reshape(q.shape)


def attention_context(h, wqkv, wo, qn, kn):
    q, k, v = attn_project(h, wqkv, qn, kn)
    o = block_attention(q, k, v)
    return o.reshape(h.shape[0], h.shape[1], Q_WIDTH) @ wo, k, v


def attention_latent(h, ctx_k, ctx_v, wqkv, wo, qn, kn):
    q, k, v = attn_project(h, wqkv, qn, kn)
    cos, sin = axial_rope_angles(h.shape[1])
    q = apply_axial_rope(q, cos, sin)
    k = apply_axial_rope(k, cos, sin)
    keys = jnp.concatenate([ctx_k.astype(k.dtype), k], axis=1)
    vals = jnp.concatenate([ctx_v.astype(v.dtype), v], axis=1)
    o = block_attention(q, keys, vals)
    return o.reshape(h.shape[0], h.shape[1], Q_WIDTH) @ wo


CUBE_PITCH = 136


def _gelu(x):
    return 0.5 * x * (1.0 + lax.erf(x * (2.0 ** -0.5)))


def _peer_experts_kernel(x_ref, a_ref, b_ref, g_ref, u_ref, v_ref, o_ref, xb_ref, cube_ref, *, tm, ib):
    e = pl.program_id(1)

    @pl.when(e == 0)
    def _build():
        xb_ref[...] = x_ref[...].astype(jnp.bfloat16)
        o_ref[...] = jnp.zeros_like(o_ref)
        row_id = lax.broadcasted_iota(jnp.int32, (N_KEYS, N_KEYS), 0)

        def per_token(t, carry):
            a_row = a_ref[pl.ds(t, 1), :]
            b_row = b_ref[pl.ds(t, 1), :]
            g_row = g_ref[pl.ds(t, 1), :]
            at = jnp.where(row_id == a_row, g_row, 0.0).astype(jnp.bfloat16)
            bt = jnp.where(row_id == b_row, 1.0, 0.0).astype(jnp.bfloat16)
            g_t = lax.dot_general(at, bt, (((1,), (1,)), ((), ())), preferred_element_type=jnp.float32)
            cube_ref[pl.ds(pl.multiple_of(t * CUBE_PITCH, 8), N_KEYS), :] = g_t
            return carry

        lax.fori_loop(0, tm, per_token, 0, unroll=8)

    xb = xb_ref[...]
    for p in range(ib // 2):
        u_pair = u_ref[pl.ds(p * 2 * N_KEYS, 2 * N_KEYS), :]
        s = lax.dot_general(xb, u_pair, (((1,), (1,)), ((), ())), preferred_element_type=jnp.float32)
        i1 = e * ib + 2 * p
        g0 = cube_ref[pl.ds(i1, tm, stride=CUBE_PITCH), :]
        g1 = cube_ref[pl.ds(i1 + 1, tm, stride=CUBE_PITCH), :]
        gates = jnp.concatenate([g0, g1], axis=1)
        h = (gates * _gelu(s)).astype(jnp.bfloat16)
        v_pair = v_ref[pl.ds(p * 2 * N_KEYS, 2 * N_KEYS), :]
        o_ref[...] += jnp.dot(h, v_pair, preferred_element_type=jnp.float32)


def peer_experts(x, a_idx, b_idx, gate, u_bf16, v_bf16, *, tm=256, ib=8):
    T, D = x.shape
    P = a_idx.shape[1]
    assert P == N_KEYS and T % tm == 0 and N_KEYS % ib == 0 and ib % 2 == 0
    kern = functools.partial(_peer_experts_kernel, tm=tm, ib=ib)
    return pl.pallas_call(
        kern,
        grid=(T // tm, N_KEYS // ib),
        in_specs=[pl.BlockSpec((tm, D), lambda i, e: (i, 0)),
                  pl.BlockSpec((tm, P), lambda i, e: (i, 0)),
                  pl.BlockSpec((tm, P), lambda i, e: (i, 0)),
                  pl.BlockSpec((tm, P), lambda i, e: (i, 0)),
                  pl.BlockSpec((ib * N_KEYS, D), lambda i, e: (e, 0)),
                  pl.BlockSpec((ib * N_KEYS, D), lambda i, e: (e, 0))],
        out_specs=pl.BlockSpec((tm, D), lambda i, e: (i, 0)),
        out_shape=jax.ShapeDtypeStruct((T, D), jnp.float32),
        scratch_shapes=[pltpu.VMEM((tm, D), jnp.bfloat16),
                        pltpu.VMEM((tm * CUBE_PITCH, N_KEYS), jnp.float32)],
        compiler_params=pltpu.CompilerParams(dimension_semantics=("parallel", "arbitrary"),
                                             vmem_limit_bytes=48 * 1024 * 1024),
        name="peer_experts",
    )(x, a_idx, b_idx, gate, u_bf16, v_bf16)


def peer_ffn(x, wq, sub_keys, u_bf16, v_bf16):
    T, D = x.shape
    q = matmul(x, wq).reshape(T, PEER_HEADS, 2, PEER_HALF)
    s = jnp.einsum('thzd,zkd->thzk', q, sub_keys).astype(jnp.float32)
    sv, si = lax.top_k(s, PEER_TOPK)
    cand = (sv[:, :, 0, :, None] + sv[:, :, 1, None, :]).reshape(T, PEER_HEADS, PEER_TOPK * PEER_TOPK)
    cv, ci = lax.top_k(cand, PEER_TOPK)
    i1 = jnp.take_along_axis(si[:, :, 0], ci // PEER_TOPK, axis=-1).reshape(T, PEER_HEADS * PEER_TOPK)
    i2 = jnp.take_along_axis(si[:, :, 1], ci % PEER_TOPK, axis=-1).reshape(T, PEER_HEADS * PEER_TOPK)
    gate = jax.nn.softmax(cv, axis=-1).reshape(T, PEER_HEADS * PEER_TOPK)
    return peer_experts(x, i1.astype(jnp.int32), i2.astype(jnp.int32), gate, u_bf16, v_bf16)


def kernel(x_prompt, x_sample, c, state_rwkv, cache_k, cache_v, c_ctx, ada_w, ada_b, ln_g, ln_b,
           rwkv_mu, rwkv_wrkv, rwkv_wo, rwkv_w0, rwkv_w1, rwkv_w2, rwkv_a0, rwkv_a1, rwkv_a2,
           rwkv_g1, rwkv_g2, rwkv_kk, rwkv_ka, rwkv_rk, rwkv_lnx_g, rwkv_lnx_b,
           attn_wqkv, attn_wo, attn_qn, attn_kn, peer_wq, peer_keys, peer_u, peer_v):
    xp, xs = x_prompt, x_sample
    states, keys_out, vals_out = [], [], []
    for i in range(DEPTH):
        j = i // 2
        mod_p = adaln_params(c_ctx[None], ada_w[i], ada_b[i])
        mod_s = adaln_params(c, ada_w[i], ada_b[i])
        hp = modulate(xp, mod_p[:, 0], mod_p[:, 1])
        hs = modulate(xs, mod_s[:, 0], mod_s[:, 1])
        if i % 2 == 0:
            rw = (rwkv_mu[j], rwkv_wrkv[j], rwkv_wo[j], rwkv_w0[j], rwkv_w1[j], rwkv_w2[j],
                  rwkv_a0[j], rwkv_a1[j], rwkv_a2[j], rwkv_g1[j], rwkv_g2[j], rwkv_kk[j], rwkv_ka[j],
                  rwkv_rk[j], rwkv_lnx_g[j], rwkv_lnx_b[j])
            s_zero = jnp.zeros((xp.shape[0], 2, RWKV_HEADS, RWKV_HEAD, RWKV_HEAD), jnp.float32)
            op, s_ctx = rwkv_time_mix(hp, s_zero, *rw)
            os_, _ = rwkv_time_mix(hs, state_rwkv[:, j], *rw)
            states.append(s_ctx)
        else:
            op, kp, vp = attention_context(hp, attn_wqkv[j], attn_wo[j], attn_qn[j], attn_kn[j])
            os_ = attention_latent(hs, cache_k[:, j], cache_v[:, j], attn_wqkv[j], attn_wo[j], attn_qn[j], attn_kn[j])
            keys_out.append(kp)
            vals_out.append(vp)
        xp = residual_post_norm(xp, mod_p[:, 2][:, None] * op, ln_g[i, 0], ln_b[i, 0])
        xs = residual_post_norm(xs, mod_s[:, 2][:, None] * os_, ln_g[i, 0], ln_b[i, 0])
        hp = modulate(xp, mod_p[:, 3], mod_p[:, 4])
        hs = modulate(xs, mod_s[:, 3], mod_s[:, 4])
        n_p = hp.shape[0] * hp.shape[1]
        h_all = jnp.concatenate([hp.reshape(n_p, D_MODEL), hs.reshape(-1, D_MODEL)], axis=0)
        f_all = peer_ffn(h_all, peer_wq[i], peer_keys[i],
                         peer_u[i].astype(jnp.bfloat16), peer_v[i].astype(jnp.bfloat16))
        fp = f_all[:n_p].reshape(hp.shape)
        fs = f_all[n_p:].reshape(hs.shape)
        xp = residual_post_norm(xp, mod_p[:, 5][:, None] * fp, ln_g[i, 1], ln_b[i, 1])
        xs = residual_post_norm(xs, mod_s[:, 5][:, None] * fs, ln_g[i, 1], ln_b[i, 1])
    return (xp, xs, jnp.stack(states, axis=1), jnp.stack(keys_out, axis=1), jnp.stack(vals_out, axis=1))
```

```python
import functools

import numpy as np
import jax
import jax.numpy as jnp
from jax import lax
from jax.experimental import pallas as pl
from jax.experimental.pallas import tpu as pltpu

D_MODEL = 1024
DEPTH = 4
GRID_W = 64
ALPHA = (2 * DEPTH) ** 0.25
LN_EPS = 1e-5
RWKV_HEAD = 64
RWKV_HEADS = D_MODEL // RWKV_HEAD
GN_EPS = RWKV_HEAD * 1e-5
HEAD_DIM = 64
N_HEADS = D_MODEL // HEAD_DIM
KV_HEADS = 4
GROUP = N_HEADS // KV_HEADS
Q_WIDTH = N_HEADS * HEAD_DIM
KV_WIDTH = KV_HEADS * HEAD_DIM
Q_BLOCK = 128
ROPE_THETA = 10000.0
ROPE_FREQS = HEAD_DIM // 4
ATTN_SCALE = HEAD_DIM ** -0.5
RMS_EPS = 1e-6
PEER_HEADS = 8
N_KEYS = 128
PEER_QUERY = 256
PEER_HALF = PEER_QUERY // 2
PEER_TOPK = 16
PEER_BLOCK = 128


def _mm_kernel(x_ref, w_ref, o_ref):
    o_ref[...] = jnp.dot(x_ref[...].astype(jnp.bfloat16), w_ref[...].astype(jnp.bfloat16),
                         preferred_element_type=jnp.float32)


def matmul(x, w, tm=512, tn=512):
    M, K = x.shape
    N = w.shape[1]
    tm = min(tm, M)
    tn = min(tn, N)
    assert M % tm == 0 and N % tn == 0, (M, N)
    return pl.pallas_call(
        _mm_kernel,
        grid=(M // tm, N // tn),
        in_specs=[pl.BlockSpec((tm, K), lambda i, j: (i, 0)),
                  pl.BlockSpec((K, tn), lambda i, j: (0, j))],
        out_specs=pl.BlockSpec((tm, tn), lambda i, j: (i, j)),
        out_shape=jax.ShapeDtypeStruct((M, N), jnp.float32),
        compiler_params=pltpu.CompilerParams(dimension_semantics=("parallel", "parallel")),
        name="matmul",
    )(x, w)


def mm3(x, w):
    return matmul(x.reshape(-1, x.shape[-1]), w).reshape(x.shape[:-1] + (w.shape[1],))


def residual_post_norm(x, branch, g, b):
    z = ALPHA * x + branch
    mu = jnp.mean(z, -1, keepdims=True)
    var = jnp.mean(jnp.square(z - mu), -1, keepdims=True)
    return (z - mu) * lax.rsqrt(var + LN_EPS) * g + b


def rms_norm(x, g):
    return x * lax.rsqrt(jnp.mean(x * x, -1, keepdims=True) + RMS_EPS) * g


def adaln_params(cond, w, b):
    return (matmul(jax.nn.silu(cond), w) + b).reshape(cond.shape[0], 6, D_MODEL)


def modulate(x, shift, scale):
    return x * (1 + scale[:, None]) + shift[:, None]


def centred_shift(x):
    prev = jnp.pad(x[:, :-1], ((0, 0), (1, 0), (0, 0)))
    nxt = jnp.pad(x[:, 1:], ((0, 0), (0, 1), (0, 0)))
    return 0.5 * (prev + nxt)


LANES = 128


def _wkv_scan_kernel(r_ref, w_ref, k_ref, v_ref, kk_ref, ka_ref, s0_ref, y_ref, sf_ref, st_ref, *, tc):
    d = pl.program_id(0)
    c = pl.program_id(2)
    n = RWKV_HEAD

    @pl.when(c == 0)
    def _init():
        st_ref[...] = s0_ref[...]

    def step(tt, carry):
        t = jnp.where(d == 0, tt, tc - 1 - tt)
        skk = jnp.zeros((n, LANES), jnp.float32)
        for j in range(n):
            skk = skk + st_ref[j] * kk_ref[t, pl.ds(j, 1), :]
        v_t = v_ref[t]
        y = jnp.zeros((n, LANES), jnp.float32)
        for j in range(n):
            s = (st_ref[j] * w_ref[t, pl.ds(j, 1), :] - skk * ka_ref[t, pl.ds(j, 1), :]
                 + v_t * k_ref[t, pl.ds(j, 1), :])
            st_ref[j] = s
            y = y + s * r_ref[t, pl.ds(j, 1), :]
        y_ref[t] = y
        return carry

    lax.fori_loop(0, tc, step, 0)

    @pl.when(c == pl.num_programs(2) - 1)
    def _fin():
        sf_ref[...] = st_ref[...]


def wkv_scan_lanes(r, w, k, v, kk, ka, s0, *, tc=16):
    T, n, L = r.shape
    assert n == RWKV_HEAD and L % LANES == 0 and T % tc == 0
    nc = T // tc
    tmap = lambda d, g, c: (jnp.where(d == 0, c, nc - 1 - c), 0, g)
    dmap = lambda d, g, c: (d, jnp.where(d == 0, c, nc - 1 - c), 0, g)
    shared = pl.BlockSpec((tc, n, LANES), tmap)
    perdir = pl.BlockSpec((None, tc, n, LANES), dmap)
    state = pl.BlockSpec((None, n, n, LANES), lambda d, g, c: (d, 0, 0, g))
    return pl.pallas_call(
        functools.partial(_wkv_scan_kernel, tc=tc),
        grid=(2, L // LANES, nc),
        in_specs=[shared, perdir, perdir, shared, shared, perdir, state],
        out_specs=[perdir, state],
        out_shape=[jax.ShapeDtypeStruct((2, T, n, L), jnp.float32),
                   jax.ShapeDtypeStruct((2, n, n, L), jnp.float32)],
        scratch_shapes=[pltpu.VMEM((n, n, LANES), jnp.float32)],
        compiler_params=pltpu.CompilerParams(dimension_semantics=("parallel", "parallel", "arbitrary")),
        name="wkv_scan",
    )(r, w, k, v, kk, ka, s0)


def _to_lanes(x):
    B, T, H, N = x.shape
    return x.transpose(1, 3, 0, 2).reshape(T, N, B * H)


def wkv_bidir(r, w, k, v, kk, a, s0):
    B, T, H, N = r.shape
    both = lambda x: jnp.stack([_to_lanes(x[0]), _to_lanes(x[1])])
    s0l = s0.transpose(1, 4, 3, 0, 2).reshape(2, N, N, B * H)
    y, sf = wkv_scan_lanes(_to_lanes(r), both(w), both(k), _to_lanes(v), _to_lanes(kk), both(kk[None] * a), s0l)
    y = (y[0] + y[1]).reshape(T, N, B, H).transpose(2, 0, 3, 1)
    sf = sf.reshape(2, N, N, B, H).transpose(3, 0, 4, 2, 1)
    return y, sf


def rwkv_time_mix(h, s0, mu, wrkv, wo, w0, w1, w2, a0, a1, a2, g1, g2, k_k, k_a, r_k, lnx_g, lnx_b):
    B, T, D = h.shape
    f32 = jnp.float32
    heads = lambda t: t.reshape(t.shape[:-1] + (RWKV_HEADS, RWKV_HEAD))
    xx = centred_shift(h) - h
    xr, xw, xk, xv, xa, xg = (h + xx * mu[m] for m in range(6))
    r = mm3(xr, wrkv[0])
    k = mm3(xk, wrkv[1])
    v = mm3(xv, wrkv[2])
    wl = w0[:, None, None, :] + jnp.stack([mm3(jnp.tanh(mm3(xw, w1[z])), w2[z]) for z in range(2)])
    decay = jnp.exp(-jnp.exp(-jax.nn.softplus(-wl.astype(f32)) - 0.5))
    a = jax.nn.sigmoid((a0[:, None, None, :] + jnp.stack([mm3(mm3(xa, a1[z]), a2[z]) for z in range(2)])).astype(f32))
    g = mm3(jax.nn.sigmoid(mm3(xg, g1)), g2)
    kk = heads((k * k_k).astype(f32))
    kk = kk / jnp.maximum(jnp.sqrt(jnp.sum(kk * kk, -1, keepdims=True)), 1e-12)
    kd = k.astype(f32)[None] * (1 + (a - 1) * k_a.astype(f32))
    rf, vf = heads(r.astype(f32)), heads(v.astype(f32))
    s0 = s0.astype(f32)
    y, s_fin = wkv_bidir(rf, heads(decay), heads(kd), vf, kk, heads(a), s0)
    ym = jnp.mean(y, -1, keepdims=True)
    yv = jnp.mean(jnp.square(y - ym), -1, keepdims=True)
    yn = ((y - ym) * lax.rsqrt(yv + GN_EPS)).reshape(B, T, D) * lnx_g + lnx_b
    bonus = jnp.sum(rf[None] * heads(kd) * r_k, axis=(0, -1))[..., None] * vf
    out = mm3((yn + bonus.reshape(B, T, D)).astype(h.dtype) * g, wo)
    return out, s_fin


def axial_rope_angles(n):
    rows = n // GRID_W
    row = jnp.repeat(jnp.arange(rows), GRID_W).astype(jnp.float32)
    col = (jnp.arange(rows * GRID_W) % GRID_W).astype(jnp.float32)
    freqs = ROPE_THETA ** (-jnp.arange(ROPE_FREQS, dtype=jnp.float32) / ROPE_FREQS)
    ang = jnp.stack([row[:, None] * freqs, col[:, None] * freqs], axis=1)
    return jnp.cos(ang), jnp.sin(ang)


def apply_axial_rope(x, cos, sin):
    xs = x.astype(jnp.float32).reshape(x.shape[:-1] + (2, 2, ROPE_FREQS))
    bshape = (x.shape[1],) + (1,) * (x.ndim - 3) + (2, ROPE_FREQS)
    cos, sin = cos.reshape(bshape), sin.reshape(bshape)
    x1, x2 = xs[..., 0, :], xs[..., 1, :]
    out = jnp.stack([x1 * cos - x2 * sin, x2 * cos + x1 * sin], axis=-2)
    return out.reshape(x.shape).astype(x.dtype)


def attn_project(h, wqkv, qn, kn):
    B, n, _ = h.shape
    qkv = mm3(h, wqkv)
    q = rms_norm(qkv[..., :Q_WIDTH].reshape(B, n, KV_HEADS, GROUP, HEAD_DIM), qn)
    k = rms_norm(qkv[..., Q_WIDTH:Q_WIDTH + KV_WIDTH].reshape(B, n, KV_HEADS, HEAD_DIM), kn)
    v = qkv[..., Q_WIDTH + KV_WIDTH:].reshape(B, n, KV_HEADS, HEAD_DIM)
    return q, k, v


def _attn_kernel(q_ref, kt_ref, v_ref, o_ref):
    kt = kt_ref[...].astype(jnp.bfloat16)
    v = v_ref[...].astype(jnp.bfloat16)
    outs = []
    for g in range(GROUP):
        qg = q_ref[:, g * HEAD_DIM:(g + 1) * HEAD_DIM].astype(jnp.bfloat16)
        s = jnp.dot(qg, kt, preferred_element_type=jnp.float32) * ATTN_SCALE
        e = jnp.exp(s - jnp.max(s, axis=-1, keepdims=True))
        l = jnp.sum(e, axis=-1, keepdims=True)
        outs.append(jnp.dot(e.astype(jnp.bfloat16), v, preferred_element_type=jnp.float32) / l)
    o_ref[...] = jnp.concatenate(outs, axis=-1)


def attention(q, k, v, *, tq=256):
    B, n = q.shape[:2]
    S = k.shape[1]
    gw = GROUP * HEAD_DIM
    return pl.pallas_call(
        _attn_kernel,
        grid=(B, KV_HEADS, n // tq),
        in_specs=[pl.BlockSpec((None, tq, gw), lambda b, h, i: (b, i, h)),
                  pl.BlockSpec((None, None, HEAD_DIM, S), lambda b, h, i: (b, h, 0, 0)),
                  pl.BlockSpec((None, None, S, HEAD_DIM), lambda b, h, i: (b, h, 0, 0))],
        out_specs=pl.BlockSpec((None, tq, gw), lambda b, h, i: (b, i, h)),
        out_shape=jax.ShapeDtypeStruct((B, n, Q_WIDTH), jnp.float32),
        compiler_params=pltpu.CompilerParams(dimension_semantics=("parallel", "parallel", "parallel")),
        name="attention",
    )(q.reshape(B, n, Q_WIDTH), k.transpose(0, 2, 3, 1), v.transpose(0, 2, 1, 3))


def attention_context(h, wqkv, wo, qn, kn):
    q, k, v = attn_project(h, wqkv, qn, kn)
    return mm3(attention(q, k, v), wo), k, v


def attention_latent(h, ctx_k, ctx_v, wqkv, wo, qn, kn):
    q, k, v = attn_project(h, wqkv, qn, kn)
    cos, sin = axial_rope_angles(h.shape[1])
    q = apply_axial_rope(q, cos, sin)
    k = apply_axial_rope(k, cos, sin)
    keys = jnp.concatenate([ctx_k.astype(k.dtype), k], axis=1)
    vals = jnp.concatenate([ctx_v.astype(v.dtype), v], axis=1)
    return mm3(attention(q, keys, vals), wo)


CUBE_PITCH = 136


def _gelu(x):
    return 0.5 * x * (1.0 + lax.erf(x * (2.0 ** -0.5)))


def _peer_experts_kernel(x_ref, a_ref, b_ref, g_ref, u_ref, v_ref, o_ref, xb_ref, cube_ref, *, tm, ib):
    e = pl.program_id(1)

    @pl.when(e == 0)
    def _build():
        xb_ref[...] = x_ref[...].astype(jnp.bfloat16)
        o_ref[...] = jnp.zeros_like(o_ref)
        row_id = lax.broadcasted_iota(jnp.int32, (N_KEYS, N_KEYS), 0)

        def per_token(t, carry):
            a_row = a_ref[pl.ds(t, 1), :]
            b_row = b_ref[pl.ds(t, 1), :]
            g_row = g_ref[pl.ds(t, 1), :]
            at = jnp.where(row_id == a_row, g_row, 0.0).astype(jnp.bfloat16)
            bt = jnp.where(row_id == b_row, 1.0, 0.0).astype(jnp.bfloat16)
            g_t = lax.dot_general(at, bt, (((1,), (1,)), ((), ())), preferred_element_type=jnp.float32)
            cube_ref[pl.ds(pl.multiple_of(t * CUBE_PITCH, 8), N_KEYS), :] = g_t
            return carry

        lax.fori_loop(0, tm, per_token, 0, unroll=8)

    xb = xb_ref[...]
    for p in range(ib // 2):
        u_pair = u_ref[pl.ds(p * 2 * N_KEYS, 2 * N_KEYS), :]
        s = lax.dot_general(xb, u_pair, (((1,), (1,)), ((), ())), preferred_element_type=jnp.float32)
        i1 = e * ib + 2 * p
        g0 = cube_ref[pl.ds(i1, tm, stride=CUBE_PITCH), :]
        g1 = cube_ref[pl.ds(i1 + 1, tm, stride=CUBE_PITCH), :]
        gates = jnp.concatenate([g0, g1], axis=1)
        h = (gates * _gelu(s)).astype(jnp.bfloat16)
        v_pair = v_ref[pl.ds(p * 2 * N_KEYS, 2 * N_KEYS), :]
        o_ref[...] += jnp.dot(h, v_pair, preferred_element_type=jnp.float32)


def peer_experts(x, a_idx, b_idx, gate, u_bf16, v_bf16, *, tm=256, ib=8):
    T, D = x.shape
    P = a_idx.shape[1]
    assert P == N_KEYS and T % tm == 0 and N_KEYS % ib == 0 and ib % 2 == 0
    kern = functools.partial(_peer_experts_kernel, tm=tm, ib=ib)
    return pl.pallas_call(
        kern,
        grid=(T // tm, N_KEYS // ib),
        in_specs=[pl.BlockSpec((tm, D), lambda i, e: (i, 0)),
                  pl.BlockSpec((tm, P), lambda i, e: (i, 0)),
                  pl.BlockSpec((tm, P), lambda i, e: (i, 0)),
                  pl.BlockSpec((tm, P), lambda i, e: (i, 0)),
                  pl.BlockSpec((ib * N_KEYS, D), lambda i, e: (e, 0)),
                  pl.BlockSpec((ib * N_KEYS, D), lambda i, e: (e, 0))],
        out_specs=pl.BlockSpec((tm, D), lambda i, e: (i, 0)),
        out_shape=jax.ShapeDtypeStruct((T, D), jnp.float32),
        scratch_shapes=[pltpu.VMEM((tm, D), jnp.bfloat16),
                        pltpu.VMEM((tm * CUBE_PITCH, N_KEYS), jnp.float32)],
        compiler_params=pltpu.CompilerParams(dimension_semantics=("parallel", "arbitrary"),
                                             vmem_limit_bytes=48 * 1024 * 1024),
        name="peer_experts",
    )(x, a_idx, b_idx, gate, u_bf16, v_bf16)


_CAND_COLS = (16, 8, 8, 8, 8, 8, 8, 8)
_CAND_ROWS = 80


def _cand_tables():
    idx = np.zeros((_CAND_ROWS, 1), np.int32)
    valid = np.zeros((_CAND_ROWS, 1), np.float32)
    pos = 0
    for r, n in enumerate(_CAND_COLS):
        for cc in range(n):
            idx[pos] = r * PEER_TOPK + cc
            valid[pos] = 1.0 if (r + 1) * (cc + 1) <= PEER_TOPK else 0.0
            pos += 1
    for r in range(8, 16):
        idx[pos] = r * PEER_TOPK
        valid[pos] = 1.0
        pos += 1
    assert pos == _CAND_ROWS
    return (jnp.asarray(np.broadcast_to(idx, (_CAND_ROWS, LANES)).copy()),
            jnp.asarray(np.broadcast_to(valid, (_CAND_ROWS, LANES)).copy()))


def _top_rows(s, ids, k, sentinel):
    vals, picks = [], []
    for _ in range(k):
        m = jnp.max(s, axis=0, keepdims=True)
        pick = jnp.min(jnp.where(s == m, ids, sentinel), axis=0, keepdims=True)
        vals.append(m)
        picks.append(pick)
        s = jnp.where(ids == pick, -jnp.inf, s)
    return jnp.concatenate(vals, axis=0), jnp.concatenate(picks, axis=0)


def _take_rows(table, rows):
    out = jnp.zeros(rows.shape, table.dtype)
    for rr in range(table.shape[0]):
        out = jnp.where(rows == rr, table[rr:rr + 1, :], out)
    return out


def _peer_route_kernel(x_ref, wqt_ref, keys_ref, cidx_ref, cvalid_ref, a_ref, b_ref, g_ref, q_ref, *, tm):
    xb = x_ref[...].astype(jnp.bfloat16)
    q_ref[...] = lax.dot_general(wqt_ref[...], xb, (((1,), (1,)), ((), ())),
                                 preferred_element_type=jnp.float32).astype(jnp.bfloat16)
    key_ids = lax.broadcasted_iota(jnp.int32, (N_KEYS, LANES), 0)

    def per_head(it, carry):
        h = it % PEER_HEADS
        lane0 = pl.multiple_of((it // PEER_HEADS) * LANES, LANES)
        sv, si = [], []
        for z in range(2):
            row0 = pl.multiple_of((h * 2 + z) * PEER_HALF, PEER_HALF)
            qhz = q_ref[pl.ds(row0, PEER_HALF), pl.ds(lane0, LANES)]
            s = jnp.dot(keys_ref[z], qhz, preferred_element_type=jnp.float32)
            v_, i_ = _top_rows(s, key_ids, PEER_TOPK, N_KEYS)
            sv.append(v_)
            si.append(i_)
        pieces = [sv[0][r:r + 1, :] + sv[1][:n, :] for r, n in enumerate(_CAND_COLS)]
        pieces.append(sv[0][8:16, :] + sv[1][0:1, :])
        cidx = cidx_ref[...]
        cand = jnp.where(cvalid_ref[...] > 0.0, jnp.concatenate(pieces, axis=0), -jnp.inf)
        cv, ci = _top_rows(cand, cidx, PEER_TOPK, PEER_TOPK * PEER_TOPK)
        i1 = _take_rows(si[0], ci >> 4)
        i2 = _take_rows(si[1], ci & (PEER_TOPK - 1))
        e = jnp.exp(cv - jnp.max(cv, axis=0, keepdims=True))
        gate = e / jnp.sum(e, axis=0, keepdims=True)
        rows = pl.ds(pl.multiple_of(h * PEER_TOPK, PEER_TOPK), PEER_TOPK)
        a_ref[rows, pl.ds(lane0, LANES)] = i1
        b_ref[rows, pl.ds(lane0, LANES)] = i2
        g_ref[rows, pl.ds(lane0, LANES)] = gate
        return carry

    lax.fori_loop(0, PEER_HEADS * (tm // LANES), per_head, 0)


def peer_route(x, wqt_bf16, keys_bf16, *, tm=256):
    T, D = x.shape
    QW = wqt_bf16.shape[0]
    P = PEER_HEADS * PEER_TOPK
    cidx, cvalid = _cand_tables()
    whole = lambda shape: pl.BlockSpec(shape, lambda i: (0,) * len(shape))
    out = pl.BlockSpec((P, tm), lambda i: (0, i))
    return pl.pallas_call(
        functools.partial(_peer_route_kernel, tm=tm),
        grid=(T // tm,),
        in_specs=[pl.BlockSpec((tm, D), lambda i: (i, 0)), whole((QW, D)), whole((2, N_KEYS, PEER_HALF)),
                  whole((_CAND_ROWS, LANES)), whole((_CAND_ROWS, LANES))],
        out_specs=[out, out, out],
        out_shape=[jax.ShapeDtypeStruct((P, T), jnp.int32), jax.ShapeDtypeStruct((P, T), jnp.int32),
                   jax.ShapeDtypeStruct((P, T), jnp.float32)],
        scratch_shapes=[pltpu.VMEM((QW, tm), jnp.bfloat16)],
        compiler_params=pltpu.CompilerParams(dimension_semantics=("parallel",)),
        name="peer_route",
    )(x, wqt_bf16, keys_bf16, cidx, cvalid)


def peer_ffn(x, wq, sub_keys, u_bf16, v_bf16):
    a, b, g = peer_route(x, wq.T.astype(jnp.bfloat16), sub_keys.astype(jnp.bfloat16))
    return peer_experts(x, a.T, b.T, g.T, u_bf16, v_bf16)


def kernel(x_prompt, x_sample, c, state_rwkv, cache_k, cache_v, c_ctx, ada_w, ada_b, ln_g, ln_b,
           rwkv_mu, rwkv_wrkv, rwkv_wo, rwkv_w0, rwkv_w1, rwkv_w2, rwkv_a0, rwkv_a1, rwkv_a2,
           rwkv_g1, rwkv_g2, rwkv_kk, rwkv_ka, rwkv_rk, rwkv_lnx_g, rwkv_lnx_b,
           attn_wqkv, attn_wo, attn_qn, attn_kn, peer_wq, peer_keys, peer_u, peer_v):
    xp, xs = x_prompt, x_sample
    states, keys_out, vals_out = [], [], []
    for i in range(DEPTH):
        j = i // 2
        mod_p = adaln_params(c_ctx[None], ada_w[i], ada_b[i])
        mod_s = adaln_params(c, ada_w[i], ada_b[i])
        hp = modulate(xp, mod_p[:, 0], mod_p[:, 1])
        hs = modulate(xs, mod_s[:, 0], mod_s[:, 1])
        if i % 2 == 0:
            rw = (rwkv_mu[j], rwkv_wrkv[j], rwkv_wo[j], rwkv_w0[j], rwkv_w1[j], rwkv_w2[j],
                  rwkv_a0[j], rwkv_a1[j], rwkv_a2[j], rwkv_g1[j], rwkv_g2[j], rwkv_kk[j], rwkv_ka[j],
                  rwkv_rk[j], rwkv_lnx_g[j], rwkv_lnx_b[j])
            s_zero = jnp.zeros((xp.shape[0], 2, RWKV_HEADS, RWKV_HEAD, RWKV_HEAD), jnp.float32)
            op, s_ctx = rwkv_time_mix(hp, s_zero, *rw)
            os_, _ = rwkv_time_mix(hs, state_rwkv[:, j], *rw)
            states.append(s_ctx)
        else:
            op, kp, vp = attention_context(hp, attn_wqkv[j], attn_wo[j], attn_qn[j], attn_kn[j])
            os_ = attention_latent(hs, cache_k[:, j], cache_v[:, j], attn_wqkv[j], attn_wo[j], attn_qn[j], attn_kn[j])
            keys_out.append(kp)
            vals_out.append(vp)
        xp = residual_post_norm(xp, mod_p[:, 2][:, None] * op, ln_g[i, 0], ln_b[i, 0])
        xs = residual_post_norm(xs, mod_s[:, 2][:, None] * os_, ln_g[i, 0], ln_b[i, 0])
        hp = modulate(xp, mod_p[:, 3], mod_p[:, 4])
        hs = modulate(xs, mod_s[:, 3], mod_s[:, 4])
        n_p = hp.shape[0] * hp.shape[1]
        h_all = jnp.concatenate([hp.reshape(n_p, D_MODEL), hs.reshape(-1, D_MODEL)], axis=0)
        f_all = peer_ffn(h_all, peer_wq[i], peer_keys[i],
                         peer_u[i].astype(jnp.bfloat16), peer_v[i].astype(jnp.bfloat16))
        fp = f_all[:n_p].reshape(hp.shape)
        fs = f_all[n_p:].reshape(hs.shape)
        xp = residual_post_norm(xp, mod_p[:, 5][:, None] * fp, ln_g[i, 1], ln_b[i, 1])
        xs = residual_post_norm(xs, mod_s[:, 5][:, None] * fs, ln_g[i, 1], ln_b[i, 1])
    return (xp, xs, jnp.stack(states, axis=1), jnp.stack(keys_out, axis=1), jnp.stack(vals_out, axis=1))
```

```python
import functools

import numpy as np
import jax
import jax.numpy as jnp
from jax import lax
from jax.experimental import pallas as pl
from jax.experimental.pallas import tpu as pltpu

D_MODEL = 1024
DEPTH = 4
GRID_W = 64
ALPHA = (2 * DEPTH) ** 0.25
LN_EPS = 1e-5
RWKV_HEAD = 64
RWKV_HEADS = D_MODEL // RWKV_HEAD
GN_EPS = RWKV_HEAD * 1e-5
HEAD_DIM = 64
N_HEADS = D_MODEL // HEAD_DIM
KV_HEADS = 4
GROUP = N_HEADS // KV_HEADS
Q_WIDTH = N_HEADS * HEAD_DIM
KV_WIDTH = KV_HEADS * HEAD_DIM
Q_BLOCK = 128
ROPE_THETA = 10000.0
ROPE_FREQS = HEAD_DIM // 4
ATTN_SCALE = HEAD_DIM ** -0.5
RMS_EPS = 1e-6
PEER_HEADS = 8
N_KEYS = 128
PEER_QUERY = 256
PEER_HALF = PEER_QUERY // 2
PEER_TOPK = 16
PEER_BLOCK = 128


def _mm_kernel(x_ref, w_ref, o_ref):
    o_ref[...] = jnp.dot(x_ref[...].astype(jnp.bfloat16), w_ref[...].astype(jnp.bfloat16),
                         preferred_element_type=jnp.float32)


def matmul(x, w, tm=512, tn=512):
    M, K = x.shape
    N = w.shape[1]
    tm = min(tm, M)
    tn = min(tn, N)
    assert M % tm == 0 and N % tn == 0, (M, N)
    return pl.pallas_call(
        _mm_kernel,
        grid=(M // tm, N // tn),
        in_specs=[pl.BlockSpec((tm, K), lambda i, j: (i, 0)),
                  pl.BlockSpec((K, tn), lambda i, j: (0, j))],
        out_specs=pl.BlockSpec((tm, tn), lambda i, j: (i, j)),
        out_shape=jax.ShapeDtypeStruct((M, N), jnp.float32),
        compiler_params=pltpu.CompilerParams(dimension_semantics=("parallel", "parallel")),
        name="matmul",
    )(x, w)


def mm3(x, w):
    return matmul(x.reshape(-1, x.shape[-1]), w).reshape(x.shape[:-1] + (w.shape[1],))


def residual_post_norm(x, branch, g, b):
    z = ALPHA * x + branch
    mu = jnp.mean(z, -1, keepdims=True)
    var = jnp.mean(jnp.square(z - mu), -1, keepdims=True)
    return (z - mu) * lax.rsqrt(var + LN_EPS) * g + b


def rms_norm(x, g):
    return x * lax.rsqrt(jnp.mean(x * x, -1, keepdims=True) + RMS_EPS) * g


def adaln_params(cond, w, b):
    return (matmul(jax.nn.silu(cond), w) + b).reshape(cond.shape[0], 6, D_MODEL)


def modulate(x, shift, scale):
    return x * (1 + scale[:, None]) + shift[:, None]


def centred_shift(x):
    prev = jnp.pad(x[:, :-1], ((0, 0), (1, 0), (0, 0)))
    nxt = jnp.pad(x[:, 1:], ((0, 0), (0, 1), (0, 0)))
    return 0.5 * (prev + nxt)


LANES = 128
SCAN_ROWS = 32
SCAN_CHAINS = 4


def _tree_sum(parts):
    while len(parts) > 1:
        parts = [parts[i] + parts[i + 1] for i in range(0, len(parts), 2)]
    return parts[0]


def _wkv_scan_kernel(r_ref, w_ref, k_ref, v_ref, kk_ref, ka_ref, s0_ref, y_ref, sf_ref, st_ref, *, tc):
    d = pl.program_id(0)
    c = pl.program_id(2)
    n = RWKV_HEAD

    @pl.when(c == 0)
    def _init():
        st_ref[...] = s0_ref[...]

    def step(tt, carry):
        t = jnp.where(d == 0, tt, tc - 1 - tt)
        for r0 in range(0, n, SCAN_ROWS):
            rows = pl.ds(r0, SCAN_ROWS)
            parts = [st_ref[j, rows, :] * kk_ref[t, pl.ds(j, 1), :] for j in range(SCAN_CHAINS)]
            for j in range(SCAN_CHAINS, n):
                parts[j % SCAN_CHAINS] = parts[j % SCAN_CHAINS] + st_ref[j, rows, :] * kk_ref[t, pl.ds(j, 1), :]
            skk = _tree_sum(parts)
            v_t = v_ref[t, rows, :]
            parts = []
            for j in range(n):
                s = (st_ref[j, rows, :] * w_ref[t, pl.ds(j, 1), :] - skk * ka_ref[t, pl.ds(j, 1), :]
                     + v_t * k_ref[t, pl.ds(j, 1), :])
                st_ref[j, rows, :] = s
                p = s * r_ref[t, pl.ds(j, 1), :]
                if j < SCAN_CHAINS:
                    parts.append(p)
                else:
                    parts[j % SCAN_CHAINS] = parts[j % SCAN_CHAINS] + p
            y_ref[t, rows, :] = _tree_sum(parts)
        return carry

    lax.fori_loop(0, tc, step, 0)

    @pl.when(c == pl.num_programs(2) - 1)
    def _fin():
        sf_ref[...] = st_ref[...]


def wkv_scan_lanes(r, w, k, v, kk, ka, s0, *, tc=16):
    T, n, L = r.shape
    assert n == RWKV_HEAD and L % LANES == 0 and T % tc == 0
    nc = T // tc
    tmap = lambda d, g, c: (jnp.where(d == 0, c, nc - 1 - c), 0, g)
    dmap = lambda d, g, c: (d, jnp.where(d == 0, c, nc - 1 - c), 0, g)
    shared = pl.BlockSpec((tc, n, LANES), tmap)
    perdir = pl.BlockSpec((None, tc, n, LANES), dmap)
    state = pl.BlockSpec((None, n, n, LANES), lambda d, g, c: (d, 0, 0, g))
    return pl.pallas_call(
        functools.partial(_wkv_scan_kernel, tc=tc),
        grid=(2, L // LANES, nc),
        in_specs=[shared, perdir, perdir, shared, shared, perdir, state],
        out_specs=[perdir, state],
        out_shape=[jax.ShapeDtypeStruct((2, T, n, L), jnp.float32),
                   jax.ShapeDtypeStruct((2, n, n, L), jnp.float32)],
        scratch_shapes=[pltpu.VMEM((n, n, LANES), jnp.float32)],
        compiler_params=pltpu.CompilerParams(dimension_semantics=("parallel", "parallel", "arbitrary")),
        name="wkv_scan",
    )(r, w, k, v, kk, ka, s0)


def _to_lanes(x):
    B, T, H, N = x.shape
    return x.transpose(1, 3, 0, 2).reshape(T, N, B * H)


def wkv_bidir(r, w, k, v, kk, a, s0):
    B, T, H, N = r.shape
    both = lambda x: jnp.stack([_to_lanes(x[0]), _to_lanes(x[1])])
    s0l = s0.transpose(1, 4, 3, 0, 2).reshape(2, N, N, B * H)
    y, sf = wkv_scan_lanes(_to_lanes(r), both(w), both(k), _to_lanes(v), _to_lanes(kk), both(kk[None] * a), s0l)
    y = (y[0] + y[1]).reshape(T, N, B, H).transpose(2, 0, 3, 1)
    sf = sf.reshape(2, N, N, B, H).transpose(3, 0, 4, 2, 1)
    return y, sf


def rwkv_time_mix(h, s0, mu, wrkv, wo, w0, w1, w2, a0, a1, a2, g1, g2, k_k, k_a, r_k, lnx_g, lnx_b):
    B, T, D = h.shape
    f32 = jnp.float32
    heads = lambda t: t.reshape(t.shape[:-1] + (RWKV_HEADS, RWKV_HEAD))
    xx = centred_shift(h) - h
    xr, xw, xk, xv, xa, xg = (h + xx * mu[m] for m in range(6))
    r = mm3(xr, wrkv[0])
    k = mm3(xk, wrkv[1])
    v = mm3(xv, wrkv[2])
    wl = w0[:, None, None, :] + jnp.stack([mm3(jnp.tanh(mm3(xw, w1[z])), w2[z]) for z in range(2)])
    decay = jnp.exp(-jnp.exp(-jax.nn.softplus(-wl.astype(f32)) - 0.5))
    a = jax.nn.sigmoid((a0[:, None, None, :] + jnp.stack([mm3(mm3(xa, a1[z]), a2[z]) for z in range(2)])).astype(f32))
    g = mm3(jax.nn.sigmoid(mm3(xg, g1)), g2)
    kk = heads((k * k_k).astype(f32))
    kk = kk / jnp.maximum(jnp.sqrt(jnp.sum(kk * kk, -1, keepdims=True)), 1e-12)
    kd = k.astype(f32)[None] * (1 + (a - 1) * k_a.astype(f32))
    rf, vf = heads(r.astype(f32)), heads(v.astype(f32))
    s0 = s0.astype(f32)
    y, s_fin = wkv_bidir(rf, heads(decay), heads(kd), vf, kk, heads(a), s0)
    ym = jnp.mean(y, -1, keepdims=True)
    yv = jnp.mean(jnp.square(y - ym), -1, keepdims=True)
    yn = ((y - ym) * lax.rsqrt(yv + GN_EPS)).reshape(B, T, D) * lnx_g + lnx_b
    bonus = jnp.sum(rf[None] * heads(kd) * r_k, axis=(0, -1))[..., None] * vf
    out = mm3((yn + bonus.reshape(B, T, D)).astype(h.dtype) * g, wo)
    return out, s_fin


def axial_rope_angles(n):
    rows = n // GRID_W
    row = jnp.repeat(jnp.arange(rows), GRID_W).astype(jnp.float32)
    col = (jnp.arange(rows * GRID_W) % GRID_W).astype(jnp.float32)
    freqs = ROPE_THETA ** (-jnp.arange(ROPE_FREQS, dtype=jnp.float32) / ROPE_FREQS)
    ang = jnp.stack([row[:, None] * freqs, col[:, None] * freqs], axis=1)
    return jnp.cos(ang), jnp.sin(ang)


def apply_axial_rope(x, cos, sin):
    xs = x.astype(jnp.float32).reshape(x.shape[:-1] + (2, 2, ROPE_FREQS))
    bshape = (x.shape[1],) + (1,) * (x.ndim - 3) + (2, ROPE_FREQS)
    cos, sin = cos.reshape(bshape), sin.reshape(bshape)
    x1, x2 = xs[..., 0, :], xs[..., 1, :]
    out = jnp.stack([x1 * cos - x2 * sin, x2 * cos + x1 * sin], axis=-2)
    return out.reshape(x.shape).astype(x.dtype)


def attn_project(h, wqkv, qn, kn):
    B, n, _ = h.shape
    qkv = mm3(h, wqkv)
    q = rms_norm(qkv[..., :Q_WIDTH].reshape(B, n, KV_HEADS, GROUP, HEAD_DIM), qn)
    k = rms_norm(qkv[..., Q_WIDTH:Q_WIDTH + KV_WIDTH].reshape(B, n, KV_HEADS, HEAD_DIM), kn)
    v = qkv[..., Q_WIDTH + KV_WIDTH:].reshape(B, n, KV_HEADS, HEAD_DIM)
    return q, k, v


def _attn_kernel(q_ref, kt_ref, v_ref, o_ref):
    kt = kt_ref[...].astype(jnp.bfloat16)
    v = v_ref[...].astype(jnp.bfloat16)
    outs = []
    for g in range(GROUP):
        qg = q_ref[:, g * HEAD_DIM:(g + 1) * HEAD_DIM].astype(jnp.bfloat16)
        s = jnp.dot(qg, kt, preferred_element_type=jnp.float32) * ATTN_SCALE
        e = jnp.exp(s - jnp.max(s, axis=-1, keepdims=True))
        l = jnp.sum(e, axis=-1, keepdims=True)
        outs.append(jnp.dot(e.astype(jnp.bfloat16), v, preferred_element_type=jnp.float32) / l)
    o_ref[...] = jnp.concatenate(outs, axis=-1)


def attention(q, k, v, *, tq=256):
    B, n = q.shape[:2]
    S = k.shape[1]
    gw = GROUP * HEAD_DIM
    return pl.pallas_call(
        _attn_kernel,
        grid=(B, KV_HEADS, n // tq),
        in_specs=[pl.BlockSpec((None, tq, gw), lambda b, h, i: (b, i, h)),
                  pl.BlockSpec((None, None, HEAD_DIM, S), lambda b, h, i: (b, h, 0, 0)),
                  pl.BlockSpec((None, None, S, HEAD_DIM), lambda b, h, i: (b, h, 0, 0))],
        out_specs=pl.BlockSpec((None, tq, gw), lambda b, h, i: (b, i, h)),
        out_shape=jax.ShapeDtypeStruct((B, n, Q_WIDTH), jnp.float32),
        compiler_params=pltpu.CompilerParams(dimension_semantics=("parallel", "parallel", "parallel")),
        name="attention",
    )(q.reshape(B, n, Q_WIDTH), k.transpose(0, 2, 3, 1), v.transpose(0, 2, 1, 3))


def attention_context(h, wqkv, wo, qn, kn):
    q, k, v = attn_project(h, wqkv, qn, kn)
    return mm3(attention(q, k, v), wo), k, v


def attention_latent(h, ctx_k, ctx_v, wqkv, wo, qn, kn):
    q, k, v = attn_project(h, wqkv, qn, kn)
    cos, sin = axial_rope_angles(h.shape[1])
    q = apply_axial_rope(q, cos, sin)
    k = apply_axial_rope(k, cos, sin)
    keys = jnp.concatenate([ctx_k.astype(k.dtype), k], axis=1)
    vals = jnp.concatenate([ctx_v.astype(v.dtype), v], axis=1)
    return mm3(attention(q, keys, vals), wo)


CUBE_PITCH = 136


def _gelu(x):
    return 0.5 * x * (1.0 + lax.erf(x * (2.0 ** -0.5)))


_HI_MASK = -65536


def _gate_tile(a_row, b_row, g_row, row_id):
    at = jnp.where(row_id == a_row, g_row, 0.0).astype(jnp.bfloat16)
    bt = jnp.where(row_id == b_row, 1.0, 0.0).astype(jnp.bfloat16)
    return lax.dot_general(at, bt, (((1,), (1,)), ((), ())), preferred_element_type=jnp.float32)


def _bf16_bits(x):
    return lax.bitcast_convert_type(x.astype(jnp.bfloat16).astype(jnp.float32), jnp.int32)


def _peer_experts_kernel(x_ref, a_ref, b_ref, g_ref, ut_ref, v_ref, o_ref, xb_ref, cube_ref, *, tm, ib):
    e = pl.program_id(1)
    half = tm // 2

    @pl.when(e == 0)
    def _build():
        xb_ref[...] = x_ref[...].astype(jnp.bfloat16)
        o_ref[...] = jnp.zeros_like(o_ref)
        row_id = lax.broadcasted_iota(jnp.int32, (N_KEYS, N_KEYS), 0)

        def per_pair(t, carry):
            t2 = t + half
            g_hi = _gate_tile(a_ref[pl.ds(t, 1), :], b_ref[pl.ds(t, 1), :], g_ref[pl.ds(t, 1), :], row_id)
            g_lo = _gate_tile(a_ref[pl.ds(t2, 1), :], b_ref[pl.ds(t2, 1), :], g_ref[pl.ds(t2, 1), :], row_id)
            word = _bf16_bits(g_hi) | lax.shift_right_logical(_bf16_bits(g_lo), 16)
            cube_ref[pl.ds(pl.multiple_of(t * CUBE_PITCH, 8), N_KEYS), :] = word
            return carry

        lax.fori_loop(0, half, per_pair, 0, unroll=8)

    s = jnp.dot(xb_ref[...], ut_ref[...], preferred_element_type=jnp.float32)
    words = jnp.concatenate([cube_ref[pl.ds(e * ib + p, half, stride=CUBE_PITCH), :] for p in range(ib)], axis=1)
    gates = jnp.concatenate([lax.bitcast_convert_type(words & _HI_MASK, jnp.float32),
                             lax.bitcast_convert_type(lax.shift_left(words, 16), jnp.float32)], axis=0)
    h = (gates * _gelu(s)).astype(jnp.bfloat16)
    o_ref[...] += jnp.dot(h, v_ref[...], preferred_element_type=jnp.float32)


def peer_experts(x, a_idx, b_idx, gate, ut_bf16, v_bf16, *, tm=512, ib=16):
    T, D = x.shape
    P = a_idx.shape[1]
    assert P == N_KEYS and T % tm == 0 and N_KEYS % ib == 0
    kern = functools.partial(_peer_experts_kernel, tm=tm, ib=ib)
    return pl.pallas_call(
        kern,
        grid=(T // tm, N_KEYS // ib),
        in_specs=[pl.BlockSpec((tm, D), lambda i, e: (i, 0)),
                  pl.BlockSpec((tm, P), lambda i, e: (i, 0)),
                  pl.BlockSpec((tm, P), lambda i, e: (i, 0)),
                  pl.BlockSpec((tm, P), lambda i, e: (i, 0)),
                  pl.BlockSpec((D, ib * N_KEYS), lambda i, e: (0, e)),
                  pl.BlockSpec((ib * N_KEYS, D), lambda i, e: (e, 0))],
        out_specs=pl.BlockSpec((tm, D), lambda i, e: (i, 0)),
        out_shape=jax.ShapeDtypeStruct((T, D), jnp.float32),
        scratch_shapes=[pltpu.VMEM((tm, D), jnp.bfloat16),
                        pltpu.VMEM((tm // 2 * CUBE_PITCH, N_KEYS), jnp.int32)],
        compiler_params=pltpu.CompilerParams(dimension_semantics=("parallel", "arbitrary"),
                                             vmem_limit_bytes=52 * 1024 * 1024),
        name="peer_experts",
    )(x, a_idx, b_idx, gate, ut_bf16, v_bf16)


_CAND_COLS = (16, 8, 8, 8, 8, 8, 8, 8)
_CAND_ROWS = 80


def _cand_tables():
    idx = np.zeros((_CAND_ROWS, 1), np.int32)
    valid = np.zeros((_CAND_ROWS, 1), np.float32)
    pos = 0
    for r, n in enumerate(_CAND_COLS):
        for cc in range(n):
            idx[pos] = r * PEER_TOPK + cc
            valid[pos] = 1.0 if (r + 1) * (cc + 1) <= PEER_TOPK else 0.0
            pos += 1
    for r in range(8, 16):
        idx[pos] = r * PEER_TOPK
        valid[pos] = 1.0
        pos += 1
    assert pos == _CAND_ROWS
    return (jnp.asarray(np.broadcast_to(idx, (_CAND_ROWS, LANES)).copy()),
            jnp.asarray(np.broadcast_to(valid, (_CAND_ROWS, LANES)).copy()))


def _top_rows(s, ids, k, sentinel):
    vals, picks = [], []
    for _ in range(k):
        m = jnp.max(s, axis=0, keepdims=True)
        pick = jnp.min(jnp.where(s == m, ids, sentinel), axis=0, keepdims=True)
        vals.append(m)
        picks.append(pick)
        s = jnp.where(ids == pick, -jnp.inf, s)
    return jnp.concatenate(vals, axis=0), jnp.concatenate(picks, axis=0)


def _take_rows(table, rows):
    out = jnp.zeros(rows.shape, table.dtype)
    for rr in range(table.shape[0]):
        out = jnp.where(rows == rr, table[rr:rr + 1, :], out)
    return out


def _peer_route_kernel(x_ref, wqt_ref, keys_ref, cidx_ref, cvalid_ref, a_ref, b_ref, g_ref, q_ref, *, tm):
    xb = x_ref[...].astype(jnp.bfloat16)
    q_ref[...] = lax.dot_general(wqt_ref[...], xb, (((1,), (1,)), ((), ())),
                                 preferred_element_type=jnp.float32).astype(jnp.bfloat16)
    key_ids = lax.broadcasted_iota(jnp.int32, (N_KEYS, LANES), 0)

    def per_head(it, carry):
        h = it % PEER_HEADS
        lane0 = pl.multiple_of((it // PEER_HEADS) * LANES, LANES)
        sv, si = [], []
        for z in range(2):
            row0 = pl.multiple_of((h * 2 + z) * PEER_HALF, PEER_HALF)
            qhz = q_ref[pl.ds(row0, PEER_HALF), pl.ds(lane0, LANES)]
            s = jnp.dot(keys_ref[z], qhz, preferred_element_type=jnp.float32)
            v_, i_ = _top_rows(s, key_ids, PEER_TOPK, N_KEYS)
            sv.append(v_)
            si.append(i_)
        pieces = [sv[0][r:r + 1, :] + sv[1][:n, :] for r, n in enumerate(_CAND_COLS)]
        pieces.append(sv[0][8:16, :] + sv[1][0:1, :])
        cidx = cidx_ref[...]
        cand = jnp.where(cvalid_ref[...] > 0.0, jnp.concatenate(pieces, axis=0), -jnp.inf)
        cv, ci = _top_rows(cand, cidx, PEER_TOPK, PEER_TOPK * PEER_TOPK)
        i1 = _take_rows(si[0], ci >> 4)
        i2 = _take_rows(si[1], ci & (PEER_TOPK - 1))
        e = jnp.exp(cv - jnp.max(cv, axis=0, keepdims=True))
        gate = e / jnp.sum(e, axis=0, keepdims=True)
        rows = pl.ds(pl.multiple_of(h * PEER_TOPK, PEER_TOPK), PEER_TOPK)
        a_ref[rows, pl.ds(lane0, LANES)] = i1
        b_ref[rows, pl.ds(lane0, LANES)] = i2
        g_ref[rows, pl.ds(lane0, LANES)] = gate
        return carry

    lax.fori_loop(0, PEER_HEADS * (tm // LANES), per_head, 0)


def peer_route(x, wqt_bf16, keys_bf16, *, tm=256):
    T, D = x.shape
    QW = wqt_bf16.shape[0]
    P = PEER_HEADS * PEER_TOPK
    cidx, cvalid = _cand_tables()
    whole = lambda shape: pl.BlockSpec(shape, lambda i: (0,) * len(shape))
    out = pl.BlockSpec((P, tm), lambda i: (0, i))
    return pl.pallas_call(
        functools.partial(_peer_route_kernel, tm=tm),
        grid=(T // tm,),
        in_specs=[pl.BlockSpec((tm, D), lambda i: (i, 0)), whole((QW, D)), whole((2, N_KEYS, PEER_HALF)),
                  whole((_CAND_ROWS, LANES)), whole((_CAND_ROWS, LANES))],
        out_specs=[out, out, out],
        out_shape=[jax.ShapeDtypeStruct((P, T), jnp.int32), jax.ShapeDtypeStruct((P, T), jnp.int32),
                   jax.ShapeDtypeStruct((P, T), jnp.float32)],
        scratch_shapes=[pltpu.VMEM((QW, tm), jnp.bfloat16)],
        compiler_params=pltpu.CompilerParams(dimension_semantics=("parallel",)),
        name="peer_route",
    )(x, wqt_bf16, keys_bf16, cidx, cvalid)


def peer_ffn(x, wq, sub_keys, ut_bf16, v_bf16):
    a, b, g = peer_route(x, wq.T.astype(jnp.bfloat16), sub_keys.astype(jnp.bfloat16))
    return peer_experts(x, a.T, b.T, g.T, ut_bf16, v_bf16)


def kernel(x_prompt, x_sample, c, state_rwkv, cache_k, cache_v, c_ctx, ada_w, ada_b, ln_g, ln_b,
           rwkv_mu, rwkv_wrkv, rwkv_wo, rwkv_w0, rwkv_w1, rwkv_w2, rwkv_a0, rwkv_a1, rwkv_a2,
           rwkv_g1, rwkv_g2, rwkv_kk, rwkv_ka, rwkv_rk, rwkv_lnx_g, rwkv_lnx_b,
           attn_wqkv, attn_wo, attn_qn, attn_kn, peer_wq, peer_keys, peer_u, peer_v):
    xp, xs = x_prompt, x_sample
    states, keys_out, vals_out = [], [], []
    for i in range(DEPTH):
        j = i // 2
        mod_p = adaln_params(c_ctx[None], ada_w[i], ada_b[i])
        mod_s = adaln_params(c, ada_w[i], ada_b[i])
        hp = modulate(xp, mod_p[:, 0], mod_p[:, 1])
        hs = modulate(xs, mod_s[:, 0], mod_s[:, 1])
        if i % 2 == 0:
            rw = (rwkv_mu[j], rwkv_wrkv[j], rwkv_wo[j], rwkv_w0[j], rwkv_w1[j], rwkv_w2[j],
                  rwkv_a0[j], rwkv_a1[j], rwkv_a2[j], rwkv_g1[j], rwkv_g2[j], rwkv_kk[j], rwkv_ka[j],
                  rwkv_rk[j], rwkv_lnx_g[j], rwkv_lnx_b[j])
            s_zero = jnp.zeros((xp.shape[0], 2, RWKV_HEADS, RWKV_HEAD, RWKV_HEAD), jnp.float32)
            op, s_ctx = rwkv_time_mix(hp, s_zero, *rw)
            os_, _ = rwkv_time_mix(hs, state_rwkv[:, j], *rw)
            states.append(s_ctx)
        else:
            op, kp, vp = attention_context(hp, attn_wqkv[j], attn_wo[j], attn_qn[j], attn_kn[j])
            os_ = attention_latent(hs, cache_k[:, j], cache_v[:, j], attn_wqkv[j], attn_wo[j], attn_qn[j], attn_kn[j])
            keys_out.append(kp)
            vals_out.append(vp)
        xp = residual_post_norm(xp, mod_p[:, 2][:, None] * op, ln_g[i, 0], ln_b[i, 0])
        xs = residual_post_norm(xs, mod_s[:, 2][:, None] * os_, ln_g[i, 0], ln_b[i, 0])
        hp = modulate(xp, mod_p[:, 3], mod_p[:, 4])
        hs = modulate(xs, mod_s[:, 3], mod_s[:, 4])
        n_p = hp.shape[0] * hp.shape[1]
        h_all = jnp.concatenate([hp.reshape(n_p, D_MODEL), hs.reshape(-1, D_MODEL)], axis=0)
        f_all = peer_ffn(h_all, peer_wq[i], peer_keys[i],
                         peer_u[i].T.astype(jnp.bfloat16), peer_v[i].astype(jnp.bfloat16))
        fp = f_all[:n_p].reshape(hp.shape)
        fs = f_all[n_p:].reshape(hs.shape)
        xp = residual_post_norm(xp, mod_p[:, 5][:, None] * fp, ln_g[i, 1], ln_b[i, 1])
        xs = residual_post_norm(xs, mod_s[:, 5][:, None] * fs, ln_g[i, 1], ln_b[i, 1])
    return (xp, xs, jnp.stack(states, axis=1), jnp.stack(keys_out, axis=1), jnp.stack(vals_out, axis=1))
```

```python
import functools

import numpy as np
import jax
import jax.numpy as jnp
from jax import lax
from jax.experimental import pallas as pl
from jax.experimental.pallas import tpu as pltpu

D_MODEL = 1024
DEPTH = 4
GRID_W = 64
ALPHA = (2 * DEPTH) ** 0.25
LN_EPS = 1e-5
RWKV_HEAD = 64
RWKV_HEADS = D_MODEL // RWKV_HEAD
DECAY_LORA = 64
GN_EPS = RWKV_HEAD * 1e-5
HEAD_DIM = 64
N_HEADS = D_MODEL // HEAD_DIM
KV_HEADS = 4
GROUP = N_HEADS // KV_HEADS
Q_WIDTH = N_HEADS * HEAD_DIM
KV_WIDTH = KV_HEADS * HEAD_DIM
ROPE_THETA = 10000.0
ROPE_FREQS = HEAD_DIM // 4
ATTN_SCALE = HEAD_DIM ** -0.5
RMS_EPS = 1e-6
PEER_HEADS = 8
N_KEYS = 128
PEER_QUERY = 256
PEER_HALF = PEER_QUERY // 2
PEER_TOPK = 16

LANES = 128
MIB = 1024 * 1024
BF16 = jnp.bfloat16
F32 = jnp.float32


def _mm_kernel(x_ref, w_ref, o_ref):
    o_ref[...] = jnp.dot(x_ref[...].astype(BF16), w_ref[...].astype(BF16), preferred_element_type=F32)


def matmul(x, w, tm=512, tn=512):
    M, K = x.shape
    N = w.shape[1]
    tm = min(tm, M)
    tn = min(tn, N)
    assert M % tm == 0 and N % tn == 0, (M, N)
    return pl.pallas_call(
        _mm_kernel,
        grid=(M // tm, N // tn),
        in_specs=[pl.BlockSpec((tm, K), lambda i, j: (i, 0)),
                  pl.BlockSpec((K, tn), lambda i, j: (0, j))],
        out_specs=pl.BlockSpec((tm, tn), lambda i, j: (i, j)),
        out_shape=jax.ShapeDtypeStruct((M, N), F32),
        compiler_params=pltpu.CompilerParams(dimension_semantics=("parallel", "parallel")),
        name="matmul",
    )(x, w)


def mm3(x, w):
    return matmul(x.reshape(-1, x.shape[-1]), w).reshape(x.shape[:-1] + (w.shape[1],))


def _bdot(a, b):
    return jnp.dot(a.astype(BF16), b, preferred_element_type=F32)


SEG_BLOCK = 256


def _seg_ones():
    i = np.arange(SEG_BLOCK) // RWKV_HEAD
    return jnp.asarray((i[:, None] == i[None, :]).astype(np.float32), dtype=BF16)


def _segsum(q, seg):
    hi = q.astype(BF16)
    lo = (q - hi.astype(F32)).astype(BF16)
    return jnp.concatenate(
        [jnp.dot(hi[:, b:b + SEG_BLOCK], seg, preferred_element_type=F32)
         + jnp.dot(lo[:, b:b + SEG_BLOCK], seg, preferred_element_type=F32)
         for b in range(0, q.shape[1], SEG_BLOCK)], axis=1)


def _layer_norm_mod(z, g, b, shift, scale):
    mu = jnp.mean(z, -1, keepdims=True)
    zc = z - mu
    var = jnp.mean(zc * zc, -1, keepdims=True)
    xn = zc * lax.rsqrt(var + LN_EPS) * g + b
    return xn, xn * (1.0 + scale) + shift


def _ln_mod_kernel(x_ref, f_ref, gate_ref, shift_ref, scale_ref, vec_ref, xn_ref, h_ref):
    z = ALPHA * x_ref[...] + gate_ref[...] * f_ref[...]
    xn, h = _layer_norm_mod(z, vec_ref[0:1, :], vec_ref[1:2, :], shift_ref[...], scale_ref[...])
    xn_ref[...] = xn
    h_ref[...] = h


def ln_mod(x, f, gate, shift, scale, ln_g, ln_b, *, tm=256):
    B, T, D = x.shape
    vec = jnp.concatenate([ln_g[None], ln_b[None], jnp.zeros((6, D), F32)], axis=0)
    tok = pl.BlockSpec((None, tm, D), lambda b, i: (b, i, 0))
    per_b = pl.BlockSpec((None, 1, D), lambda b, i: (b, 0, 0))
    o1 = jax.ShapeDtypeStruct((B, T, D), F32)
    return pl.pallas_call(
        _ln_mod_kernel,
        grid=(B, T // tm),
        in_specs=[tok, tok, per_b, per_b, per_b, pl.BlockSpec((8, D), lambda b, i: (0, 0))],
        out_specs=[tok, tok],
        out_shape=[o1, o1],
        compiler_params=pltpu.CompilerParams(dimension_semantics=("parallel", "parallel")),
        name="ln_mod",
    )(x, f, gate, shift, scale, vec)


def _softplus(x):
    return jnp.maximum(x, 0.0) + jnp.log1p(jnp.exp(-jnp.abs(x)))


def _rwkv_pre_kernel(x_ref, xp_ref, xn_ref, shift_ref, scale_ref, mu_ref, vec_ref, wr_ref, wk_ref, wv_ref,
                     l1_ref, w2_ref, a2_ref, g2_ref, seg_ref,
                     r_ref, v_ref, kk_ref, w_ref, kd_ref, ka_ref, g_ref, bonus_ref, *, tm):
    i = pl.program_id(1)
    last = pl.num_programs(1) - 1
    one_scale = 1.0 + scale_ref[...]
    shift = shift_ref[...]
    h = x_ref[...] * one_scale + shift
    h_before = jnp.where(i == 0, 0.0, xp_ref[7:8, :] * one_scale + shift)
    h_after = jnp.where(i == last, 0.0, xn_ref[0:1, :] * one_scale + shift)
    row = lax.broadcasted_iota(jnp.int32, (tm, 1), 0)
    prev = jnp.where(row == 0, h_before, pltpu.roll(h, 1, axis=0))
    nxt = jnp.where(row == tm - 1, h_after, pltpu.roll(h, tm - 1, axis=0))
    xx = 0.5 * (prev + nxt) - h
    mix = lambda m: (h + xx * mu_ref[m:m + 1, :]).astype(BF16)
    w0, a0 = vec_ref[0:2, :], vec_ref[2:4, :]
    k_k, k_a, r_k = vec_ref[4:5, :], vec_ref[5:6, :], vec_ref[6:7, :]
    seg = seg_ref[...]
    r = jnp.dot(mix(0), wr_ref[...], preferred_element_type=F32)
    k = jnp.dot(mix(2), wk_ref[...], preferred_element_type=F32)
    v = jnp.dot(mix(3), wv_ref[...], preferred_element_type=F32)
    lw = jnp.tanh(jnp.dot(mix(1), l1_ref[0], preferred_element_type=F32))
    la = jnp.dot(mix(4), l1_ref[1], preferred_element_type=F32)
    lg = jax.nn.sigmoid(jnp.dot(mix(5), l1_ref[2], preferred_element_type=F32))
    g_ref[...] = _bdot(lg, g2_ref[...])
    kkr = k * k_k
    kk = kkr / jnp.maximum(jnp.sqrt(_segsum(kkr * kkr, seg)), 1e-12)
    r_ref[...] = r
    v_ref[...] = v
    kk_ref[...] = kk
    kd_sum = jnp.zeros_like(k)
    for z in range(2):
        half = slice(z * DECAY_LORA, (z + 1) * DECAY_LORA)
        wl = w0[z:z + 1, :] + _bdot(lw[:, half], w2_ref[z])
        w_ref[z] = jnp.exp(-jnp.exp(-_softplus(-wl) - 0.5))
        a = jax.nn.sigmoid(a0[z:z + 1, :] + _bdot(la[:, half], a2_ref[z]))
        kd = k * (1.0 + (a - 1.0) * k_a)
        kd_ref[z] = kd
        ka_ref[z] = kk * a
        kd_sum = kd_sum + r * kd * r_k
    bonus_ref[...] = _segsum(kd_sum, seg) * v


def rwkv_pre(x, shift, scale, mu, wrkv, w0, w1, w2, a0, a1, a2, g1, g2, k_k, k_a, r_k, *, tm=128):
    B, T, D = x.shape
    nb = tm // 8
    vec = jnp.concatenate([w0, a0, k_k[None], k_a[None], r_k.reshape(1, D), jnp.zeros((1, D), F32)], axis=0)
    l1 = jnp.stack([jnp.concatenate([w1[0], w1[1]], axis=1), jnp.concatenate([a1[0], a1[1]], axis=1), g1]).astype(BF16)
    tok = pl.BlockSpec((None, tm, D), lambda b, i: (b, i, 0))
    tok2 = pl.BlockSpec((2, None, tm, D), lambda b, i: (0, b, i, 0))
    whole = lambda a: pl.BlockSpec(a.shape, lambda b, i: (0,) * a.ndim)
    per_b = pl.BlockSpec((None, 1, D), lambda b, i: (b, 0, 0))
    consts = [mu, vec, wrkv[0].astype(BF16), wrkv[1].astype(BF16), wrkv[2].astype(BF16), l1,
              w2.astype(BF16), a2.astype(BF16), g2.astype(BF16), _seg_ones()]
    o1 = jax.ShapeDtypeStruct((B, T, D), F32)
    o2 = jax.ShapeDtypeStruct((2, B, T, D), F32)
    return pl.pallas_call(
        functools.partial(_rwkv_pre_kernel, tm=tm),
        grid=(B, T // tm),
        in_specs=[tok,
                  pl.BlockSpec((None, 8, D), lambda b, i: (b, jnp.maximum(i * nb - 1, 0), 0)),
                  pl.BlockSpec((None, 8, D), lambda b, i: (b, jnp.minimum((i + 1) * nb, T // 8 - 1), 0)),
                  per_b, per_b] + [whole(a) for a in consts],
        out_specs=[tok, tok, tok, tok2, tok2, tok2, tok, tok],
        out_shape=[o1, o1, o1, o2, o2, o2, o1, o1],
        compiler_params=pltpu.CompilerParams(dimension_semantics=("parallel", "parallel"),
                                             vmem_limit_bytes=48 * MIB),
        name="rwkv_pre",
    )(x, x, x, shift, scale, *consts)


def _rwkv_post_kernel(y_ref, bonus_ref, g_ref, x_ref, gate_ref, shift_ref, scale_ref, vec_ref, wo_ref, seg_ref,
                      xn_ref, h_ref):
    seg = seg_ref[...]
    y = y_ref[0] + y_ref[1]
    yc = y - _segsum(y, seg) * (1.0 / RWKV_HEAD)
    yv = _segsum(yc * yc, seg) * (1.0 / RWKV_HEAD)
    yn = yc * lax.rsqrt(yv + GN_EPS) * vec_ref[0:1, :] + vec_ref[1:2, :]
    o = _bdot((yn + bonus_ref[...]) * g_ref[...], wo_ref[...])
    z = ALPHA * x_ref[...] + gate_ref[...] * o
    xn, h = _layer_norm_mod(z, vec_ref[2:3, :], vec_ref[3:4, :], shift_ref[...], scale_ref[...])
    xn_ref[...] = xn
    h_ref[...] = h


def rwkv_post(y, bonus, g, x, gate, shift, scale, lnx_g, lnx_b, wo, ln_g, ln_b, *, tm=256):
    B, T, D = x.shape
    vec = jnp.concatenate([lnx_g[None], lnx_b[None], ln_g[None], ln_b[None], jnp.zeros((4, D), F32)], axis=0)
    tok = pl.BlockSpec((None, tm, D), lambda b, i: (b, i, 0))
    per_b = pl.BlockSpec((None, 1, D), lambda b, i: (b, 0, 0))
    whole = lambda a: pl.BlockSpec(a.shape, lambda b, i: (0,) * a.ndim)
    consts = [vec, wo.astype(BF16), _seg_ones()]
    o1 = jax.ShapeDtypeStruct((B, T, D), F32)
    return pl.pallas_call(
        _rwkv_post_kernel,
        grid=(B, T // tm),
        in_specs=[pl.BlockSpec((2, None, tm, D), lambda b, i: (0, b, i, 0)), tok, tok, tok, per_b, per_b, per_b]
                 + [whole(a) for a in consts],
        out_specs=[tok, tok],
        out_shape=[o1, o1],
        compiler_params=pltpu.CompilerParams(dimension_semantics=("parallel", "parallel")),
        name="rwkv_post",
    )(y, bonus, g, x, gate, shift, scale, *consts)


SCAN_ROWS = 32
SCAN_CHAINS = 4
SCAN_BATCH = 8
HEAD_PAIRS = D_MODEL // LANES


def _tree_sum(parts):
    while len(parts) > 1:
        parts = [parts[i] + parts[i + 1] for i in range(0, len(parts), 2)]
    return parts[0]


def _to_chain_tiles(src_ref, t0):
    z = jnp.concatenate([src_ref[:, pl.ds((t0 + s) * HEAD_PAIRS, HEAD_PAIRS), :].reshape(SCAN_BATCH * HEAD_PAIRS, LANES)
                         for s in range(2)], axis=0)
    zt = z.T
    top, bot = zt[:RWKV_HEAD], zt[RWKV_HEAD:]
    low = lax.broadcasted_iota(jnp.int32, (RWKV_HEAD, LANES), 1) < RWKV_HEAD
    return (jnp.where(low, top, pltpu.roll(bot, RWKV_HEAD, axis=1)),
            jnp.where(low, pltpu.roll(top, RWKV_HEAD, axis=1), bot))


def _from_chain_tiles(y0, y1):
    low = lax.broadcasted_iota(jnp.int32, (RWKV_HEAD, LANES), 1) < RWKV_HEAD
    top = jnp.where(low, y0, pltpu.roll(y1, RWKV_HEAD, axis=1))
    bot = jnp.where(low, pltpu.roll(y0, RWKV_HEAD, axis=1), y1)
    return jnp.concatenate([top, bot], axis=0).T


def _wkv_scan_kernel(r_ref, w_ref, k_ref, v_ref, kk_ref, ka_ref, s0_ref, y_ref, sf_ref,
                     st_ref, rs, ws, ks_, vs, kks, kas, *, tc):
    d = pl.program_id(0)
    c = pl.program_id(2)
    n = RWKV_HEAD
    npairs = tc // 2
    srcs = (r_ref, w_ref, k_ref, v_ref, kk_ref, ka_ref)
    dsts = (rs, ws, ks_, vs, kks, kas)
    fwd = d == 0

    @pl.when(c == 0)
    def _init():
        st_ref[...] = s0_ref[...]

    def pair_start(p):
        return 2 * jnp.where(fwd, p, npairs - 1 - p)

    def stage(p, slot):
        t0 = pair_start(jnp.minimum(p, npairs - 1))
        for src, dst in zip(srcs, dsts):
            a, b = _to_chain_tiles(src, t0)
            dst[2 * slot] = jnp.where(fwd, a, b)
            dst[2 * slot + 1] = jnp.where(fwd, b, a)

    def one_step(i):
        halves = []
        for r0 in range(0, n, SCAN_ROWS):
            rows = pl.ds(r0, SCAN_ROWS)
            parts = [st_ref[j, rows, :] * kks[i, pl.ds(j, 1), :] for j in range(SCAN_CHAINS)]
            for j in range(SCAN_CHAINS, n):
                parts[j % SCAN_CHAINS] = parts[j % SCAN_CHAINS] + st_ref[j, rows, :] * kks[i, pl.ds(j, 1), :]
            skk = _tree_sum(parts)
            v_t = vs[i, rows, :]
            parts = []
            for j in range(n):
                s = (st_ref[j, rows, :] * ws[i, pl.ds(j, 1), :] - skk * kas[i, pl.ds(j, 1), :]
                     + v_t * ks_[i, pl.ds(j, 1), :])
                st_ref[j, rows, :] = s
                p = s * rs[i, pl.ds(j, 1), :]
                if j < SCAN_CHAINS:
                    parts.append(p)
                else:
                    parts[j % SCAN_CHAINS] = parts[j % SCAN_CHAINS] + p
            halves.append(_tree_sum(parts))
        return jnp.concatenate(halves, axis=0)

    def run_pair(p, slot):
        stage(p + 1, 1 - slot)
        y_a = one_step(2 * slot)
        y_b = one_step(2 * slot + 1)
        out = _from_chain_tiles(jnp.where(fwd, y_a, y_b), jnp.where(fwd, y_b, y_a))
        t0 = pair_start(p)
        for s in range(2):
            y_ref[:, pl.ds((t0 + s) * HEAD_PAIRS, HEAD_PAIRS), :] = out[s * 64:(s + 1) * 64].reshape(
                SCAN_BATCH, HEAD_PAIRS, LANES)

    stage(0, 0)

    def two_pairs(q, carry):
        run_pair(2 * q, 0)
        run_pair(2 * q + 1, 1)
        return carry

    lax.fori_loop(0, npairs // 2, two_pairs, 0)

    @pl.when(c == pl.num_programs(2) - 1)
    def _fin():
        sf_ref[...] = st_ref[...]


def wkv_scan_tokens(r, w, k, v, kk, ka, s0, *, tc=16):
    B, T, D = r.shape
    n = RWKV_HEAD
    G = B // SCAN_BATCH
    assert D == HEAD_PAIRS * LANES and B % SCAN_BATCH == 0 and T % tc == 0 and tc % 4 == 0
    nc = T // tc
    rows = lambda x: x.reshape(x.shape[:-2] + (T * HEAD_PAIRS, LANES))
    cm = lambda d, c: jnp.where(d == 0, c, nc - 1 - c)
    shared = pl.BlockSpec((SCAN_BATCH, tc * HEAD_PAIRS, LANES), lambda d, g, c: (g, cm(d, c), 0))
    perdir = pl.BlockSpec((None, SCAN_BATCH, tc * HEAD_PAIRS, LANES), lambda d, g, c: (d, g, cm(d, c), 0))
    state = pl.BlockSpec((None, None, n, n, LANES), lambda d, g, c: (d, g, 0, 0, 0))
    tile = pltpu.VMEM((4, n, LANES), F32)
    y, sf = pl.pallas_call(
        functools.partial(_wkv_scan_kernel, tc=tc),
        grid=(2, G, nc),
        in_specs=[shared, perdir, perdir, shared, shared, perdir, state],
        out_specs=[perdir, state],
        out_shape=[jax.ShapeDtypeStruct((2, B, T * HEAD_PAIRS, LANES), F32),
                   jax.ShapeDtypeStruct((2, G, n, n, LANES), F32)],
        scratch_shapes=[pltpu.VMEM((n, n, LANES), F32)] + [tile] * 6,
        compiler_params=pltpu.CompilerParams(dimension_semantics=("parallel", "parallel", "arbitrary")),
        name="wkv_scan",
    )(rows(r), rows(w), rows(k), rows(v), rows(kk), rows(ka), s0)
    return y.reshape(2, B, T, D), sf


def state_to_chains(s):
    B = s.shape[0]
    s = s.reshape(B // SCAN_BATCH, SCAN_BATCH, 2, HEAD_PAIRS, 2, RWKV_HEAD, RWKV_HEAD)
    return s.transpose(2, 0, 6, 5, 4, 1, 3).reshape(2, B // SCAN_BATCH, RWKV_HEAD, RWKV_HEAD, LANES)


def state_from_chains(s):
    G = s.shape[1]
    s = s.reshape(2, G, RWKV_HEAD, RWKV_HEAD, 2, SCAN_BATCH, HEAD_PAIRS)
    return s.transpose(1, 5, 0, 6, 4, 3, 2).reshape(G * SCAN_BATCH, 2, 2 * HEAD_PAIRS, RWKV_HEAD, RWKV_HEAD)


def rwkv_sublayer(x, mod, s0_chains, ln_g, ln_b, mu, wrkv, wo, w0, w1, w2, a0, a1, a2, g1, g2, k_k, k_a, r_k,
                  lnx_g, lnx_b):
    r, v, kk, w, kd, ka, g, bonus = rwkv_pre(x, mod[:, 0:1], mod[:, 1:2], mu, wrkv, w0, w1, w2, a0, a1, a2,
                                             g1, g2, k_k, k_a, r_k)
    y, s_fin = wkv_scan_tokens(r, w, kd, v, kk, ka, s0_chains)
    xn, h = rwkv_post(y, bonus, g, x, mod[:, 2:3], mod[:, 3:4], mod[:, 4:5], lnx_g, lnx_b, wo, ln_g, ln_b)
    return xn, h, s_fin


def rms_norm(x, g):
    return x * lax.rsqrt(jnp.mean(x * x, -1, keepdims=True) + RMS_EPS) * g


def axial_rope_angles(n):
    rows = n // GRID_W
    row = jnp.repeat(jnp.arange(rows), GRID_W).astype(F32)
    col = (jnp.arange(rows * GRID_W) % GRID_W).astype(F32)
    freqs = ROPE_THETA ** (-jnp.arange(ROPE_FREQS, dtype=F32) / ROPE_FREQS)
    ang = jnp.stack([row[:, None] * freqs, col[:, None] * freqs], axis=1)
    return jnp.cos(ang), jnp.sin(ang)


def apply_axial_rope(x, cos, sin):
    xs = x.reshape(x.shape[:-1] + (2, 2, ROPE_FREQS))
    bshape = (x.shape[1],) + (1,) * (x.ndim - 3) + (2, ROPE_FREQS)
    cos, sin = cos.reshape(bshape), sin.reshape(bshape)
    x1, x2 = xs[..., 0, :], xs[..., 1, :]
    out = jnp.stack([x1 * cos - x2 * sin, x2 * cos + x1 * sin], axis=-2)
    return out.reshape(x.shape)


def attn_project(h, wqkv, qn, kn):
    B, n, _ = h.shape
    qkv = mm3(h, wqkv)
    q = rms_norm(qkv[..., :Q_WIDTH].reshape(B, n, KV_HEADS, GROUP, HEAD_DIM), qn)
    k = rms_norm(qkv[..., Q_WIDTH:Q_WIDTH + KV_WIDTH].reshape(B, n, KV_HEADS, HEAD_DIM), kn)
    v = qkv[..., Q_WIDTH + KV_WIDTH:].reshape(B, n, KV_HEADS, HEAD_DIM)
    return q, k, v


def _attn_kernel(q_ref, kt_ref, v_ref, o_ref):
    kt = kt_ref[...].astype(BF16)
    v = v_ref[...].astype(BF16)
    outs = []
    for g in range(GROUP):
        qg = q_ref[:, g * HEAD_DIM:(g + 1) * HEAD_DIM].astype(BF16)
        s = jnp.dot(qg, kt, preferred_element_type=F32) * ATTN_SCALE
        e = jnp.exp(s - jnp.max(s, axis=-1, keepdims=True))
        l = jnp.sum(e, axis=-1, keepdims=True)
        outs.append(jnp.dot(e.astype(BF16), v, preferred_element_type=F32) / l)
    o_ref[...] = jnp.concatenate(outs, axis=-1)


def attention(q, k, v, *, tq=256):
    B, n = q.shape[:2]
    S = k.shape[1]
    gw = GROUP * HEAD_DIM
    return pl.pallas_call(
        _attn_kernel,
        grid=(B, KV_HEADS, n // tq),
        in_specs=[pl.BlockSpec((None, tq, gw), lambda b, h, i: (b, i, h)),
                  pl.BlockSpec((None, None, HEAD_DIM, S), lambda b, h, i: (b, h, 0, 0)),
                  pl.BlockSpec((None, None, S, HEAD_DIM), lambda b, h, i: (b, h, 0, 0))],
        out_specs=pl.BlockSpec((None, tq, gw), lambda b, h, i: (b, i, h)),
        out_shape=jax.ShapeDtypeStruct((B, n, Q_WIDTH), F32),
        compiler_params=pltpu.CompilerParams(dimension_semantics=("parallel", "parallel", "parallel")),
        name="attention",
    )(q.reshape(B, n, Q_WIDTH), k.transpose(0, 2, 3, 1), v.transpose(0, 2, 1, 3))


def attention_context(h, wqkv, wo, qn, kn):
    q, k, v = attn_project(h, wqkv, qn, kn)
    return mm3(attention(q, k, v), wo), k, v


def attention_latent(h, ctx_k, ctx_v, wqkv, wo, qn, kn):
    q, k, v = attn_project(h, wqkv, qn, kn)
    cos, sin = axial_rope_angles(h.shape[1])
    q = apply_axial_rope(q, cos, sin)
    k = apply_axial_rope(k, cos, sin)
    keys = jnp.concatenate([ctx_k, k], axis=1)
    vals = jnp.concatenate([ctx_v, v], axis=1)
    return mm3(attention(q, keys, vals), wo)


CUBE_PITCH = 136
_HI_MASK = -65536


def _gelu(x):
    return 0.5 * x * (1.0 + lax.erf(x * (2.0 ** -0.5)))


def _gate_tile(a_row, b_row, g_row, row_id):
    at = jnp.where(row_id == a_row, g_row, 0.0).astype(BF16)
    bt = jnp.where(row_id == b_row, 1.0, 0.0).astype(BF16)
    return lax.dot_general(at, bt, (((1,), (1,)), ((), ())), preferred_element_type=F32)


def _bf16_bits(x):
    return lax.bitcast_convert_type(x.astype(BF16).astype(F32), jnp.int32)


def _peer_experts_kernel(x_ref, a_ref, b_ref, g_ref, ut_ref, v_ref, o_ref, xb_ref, cube_ref, *, tm, ib):
    e = pl.program_id(1)
    half = tm // 2

    @pl.when(e == 0)
    def _build():
        xb_ref[...] = x_ref[...].astype(BF16)
        o_ref[...] = jnp.zeros_like(o_ref)
        row_id = lax.broadcasted_iota(jnp.int32, (N_KEYS, N_KEYS), 0)

        def per_pair(t, carry):
            t2 = t + half
            g_hi = _gate_tile(a_ref[pl.ds(t, 1), :], b_ref[pl.ds(t, 1), :], g_ref[pl.ds(t, 1), :], row_id)
            g_lo = _gate_tile(a_ref[pl.ds(t2, 1), :], b_ref[pl.ds(t2, 1), :], g_ref[pl.ds(t2, 1), :], row_id)
            word = _bf16_bits(g_hi) | lax.shift_right_logical(_bf16_bits(g_lo), 16)
            cube_ref[pl.ds(pl.multiple_of(t * CUBE_PITCH, 8), N_KEYS), :] = word
            return carry

        lax.fori_loop(0, half, per_pair, 0, unroll=8)

    s = jnp.dot(xb_ref[...], ut_ref[...], preferred_element_type=F32)
    words = jnp.concatenate([cube_ref[pl.ds(e * ib + p, half, stride=CUBE_PITCH), :] for p in range(ib)], axis=1)
    gates = jnp.concatenate([lax.bitcast_convert_type(words & _HI_MASK, F32),
                             lax.bitcast_convert_type(lax.shift_left(words, 16), F32)], axis=0)
    h = (gates * _gelu(s)).astype(BF16)
    o_ref[...] += jnp.dot(h, v_ref[...], preferred_element_type=F32)


def peer_experts(x, a_idx, b_idx, gate, ut_bf16, v_bf16, *, tm=512, ib=16):
    T, D = x.shape
    P = a_idx.shape[1]
    assert P == N_KEYS and T % tm == 0 and N_KEYS % ib == 0
    kern = functools.partial(_peer_experts_kernel, tm=tm, ib=ib)
    return pl.pallas_call(
        kern,
        grid=(T // tm, N_KEYS // ib),
        in_specs=[pl.BlockSpec((tm, D), lambda i, e: (i, 0)),
                  pl.BlockSpec((tm, P), lambda i, e: (i, 0)),
                  pl.BlockSpec((tm, P), lambda i, e: (i, 0)),
                  pl.BlockSpec((tm, P), lambda i, e: (i, 0)),
                  pl.BlockSpec((D, ib * N_KEYS), lambda i, e: (0, e)),
                  pl.BlockSpec((ib * N_KEYS, D), lambda i, e: (e, 0))],
        out_specs=pl.BlockSpec((tm, D), lambda i, e: (i, 0)),
        out_shape=jax.ShapeDtypeStruct((T, D), F32),
        scratch_shapes=[pltpu.VMEM((tm, D), BF16),
                        pltpu.VMEM((tm // 2 * CUBE_PITCH, N_KEYS), jnp.int32)],
        compiler_params=pltpu.CompilerParams(dimension_semantics=("parallel", "arbitrary"),
                                             vmem_limit_bytes=52 * MIB),
        name="peer_experts",
    )(x, a_idx, b_idx, gate, ut_bf16, v_bf16)


_CAND_COLS = (16, 8, 8, 8, 8, 8, 8, 8)
_CAND_ROWS = 80


def _cand_tables():
    idx = np.zeros((_CAND_ROWS, 1), np.int32)
    valid = np.zeros((_CAND_ROWS, 1), np.float32)
    pos = 0
    for r, n in enumerate(_CAND_COLS):
        for cc in range(n):
            idx[pos] = r * PEER_TOPK + cc
            valid[pos] = 1.0 if (r + 1) * (cc + 1) <= PEER_TOPK else 0.0
            pos += 1
    for r in range(8, 16):
        idx[pos] = r * PEER_TOPK
        valid[pos] = 1.0
        pos += 1
    assert pos == _CAND_ROWS
    return (jnp.asarray(np.broadcast_to(idx, (_CAND_ROWS, LANES)).copy()),
            jnp.asarray(np.broadcast_to(valid, (_CAND_ROWS, LANES)).copy()))


def _top_rows(s, ids, k, sentinel):
    vals, picks = [], []
    for _ in range(k):
        m = jnp.max(s, axis=0, keepdims=True)
        pick = jnp.min(jnp.where(s == m, ids, sentinel), axis=0, keepdims=True)
        vals.append(m)
        picks.append(pick)
        s = jnp.where(ids == pick, -jnp.inf, s)
    return jnp.concatenate(vals, axis=0), jnp.concatenate(picks, axis=0)


def _take_rows(table, rows):
    out = jnp.zeros(rows.shape, table.dtype)
    for rr in range(table.shape[0]):
        out = jnp.where(rows == rr, table[rr:rr + 1, :], out)
    return out


def _peer_route_kernel(x_ref, wqt_ref, keys_ref, cidx_ref, cvalid_ref, a_ref, b_ref, g_ref, q_ref, *, tm):
    xb = x_ref[...].astype(BF16)
    q_ref[...] = lax.dot_general(wqt_ref[...], xb, (((1,), (1,)), ((), ())),
                                 preferred_element_type=F32).astype(BF16)
    key_ids = lax.broadcasted_iota(jnp.int32, (N_KEYS, LANES), 0)

    def per_head(it, carry):
        h = it % PEER_HEADS
        lane0 = pl.multiple_of((it // PEER_HEADS) * LANES, LANES)
        sv, si = [], []
        for z in range(2):
            row0 = pl.multiple_of((h * 2 + z) * PEER_HALF, PEER_HALF)
            qhz = q_ref[pl.ds(row0, PEER_HALF), pl.ds(lane0, LANES)]
            s = jnp.dot(keys_ref[z], qhz, preferred_element_type=F32)
            v_, i_ = _top_rows(s, key_ids, PEER_TOPK, N_KEYS)
            sv.append(v_)
            si.append(i_)
        pieces = [sv[0][r:r + 1, :] + sv[1][:n, :] for r, n in enumerate(_CAND_COLS)]
        pieces.append(sv[0][8:16, :] + sv[1][0:1, :])
        cidx = cidx_ref[...]
        cand = jnp.where(cvalid_ref[...] > 0.0, jnp.concatenate(pieces, axis=0), -jnp.inf)
        cv, ci = _top_rows(cand, cidx, PEER_TOPK, PEER_TOPK * PEER_TOPK)
        i1 = _take_rows(si[0], ci >> 4)
        i2 = _take_rows(si[1], ci & (PEER_TOPK - 1))
        e = jnp.exp(cv - jnp.max(cv, axis=0, keepdims=True))
        gate = e / jnp.sum(e, axis=0, keepdims=True)
        rows = pl.ds(pl.multiple_of(h * PEER_TOPK, PEER_TOPK), PEER_TOPK)
        a_ref[rows, pl.ds(lane0, LANES)] = i1
        b_ref[rows, pl.ds(lane0, LANES)] = i2
        g_ref[rows, pl.ds(lane0, LANES)] = gate
        return carry

    lax.fori_loop(0, PEER_HEADS * (tm // LANES), per_head, 0)


def peer_route(x, wqt_bf16, keys_bf16, *, tm=256):
    T, D = x.shape
    QW = wqt_bf16.shape[0]
    P = PEER_HEADS * PEER_TOPK
    cidx, cvalid = _cand_tables()
    whole = lambda shape: pl.BlockSpec(shape, lambda i: (0,) * len(shape))
    out = pl.BlockSpec((P, tm), lambda i: (0, i))
    return pl.pallas_call(
        functools.partial(_peer_route_kernel, tm=tm),
        grid=(T // tm,),
        in_specs=[pl.BlockSpec((tm, D), lambda i: (i, 0)), whole((QW, D)), whole((2, N_KEYS, PEER_HALF)),
                  whole((_CAND_ROWS, LANES)), whole((_CAND_ROWS, LANES))],
        out_specs=[out, out, out],
        out_shape=[jax.ShapeDtypeStruct((P, T), jnp.int32), jax.ShapeDtypeStruct((P, T), jnp.int32),
                   jax.ShapeDtypeStruct((P, T), F32)],
        scratch_shapes=[pltpu.VMEM((QW, tm), BF16)],
        compiler_params=pltpu.CompilerParams(dimension_semantics=("parallel",)),
        name="peer_route",
    )(x, wqt_bf16, keys_bf16, cidx, cvalid)


def peer_ffn(x, wq, sub_keys, ut_bf16, v_bf16):
    a, b, g = peer_route(x, wq.T.astype(BF16), sub_keys.astype(BF16))
    return peer_experts(x, a.T, b.T, g.T, ut_bf16, v_bf16)


def kernel(x_prompt, x_sample, c, state_rwkv, cache_k, cache_v, c_ctx, ada_w, ada_b, ln_g, ln_b,
           rwkv_mu, rwkv_wrkv, rwkv_wo, rwkv_w0, rwkv_w1, rwkv_w2, rwkv_a0, rwkv_a1, rwkv_a2,
           rwkv_g1, rwkv_g2, rwkv_kk, rwkv_ka, rwkv_rk, rwkv_lnx_g, rwkv_lnx_b,
           attn_wqkv, attn_wo, attn_qn, attn_kn, peer_wq, peer_keys, peer_u, peer_v):
    Bp, Tp, D = x_prompt.shape
    Bs, Ts, _ = x_sample.shape
    n_p = Bp * Tp
    cond = jax.nn.silu(jnp.concatenate([c_ctx[None], c], axis=0))
    mods = []
    for i in range(DEPTH):
        m = (matmul(cond, ada_w[i]) + ada_b[i]).reshape(1 + Bs, 6, D)
        mods.append((jnp.broadcast_to(m[0:1], (Bp, 6, D)), m[1:]))
    xp, xs = x_prompt, x_sample
    hp = hs = None
    states, keys_out, vals_out = [], [], []
    for i in range(DEPTH):
        j = i // 2
        mod_p, mod_s = mods[i]
        if i % 2 == 0:
            rw = (rwkv_mu[j], rwkv_wrkv[j], rwkv_wo[j], rwkv_w0[j], rwkv_w1[j], rwkv_w2[j],
                  rwkv_a0[j], rwkv_a1[j], rwkv_a2[j], rwkv_g1[j], rwkv_g2[j], rwkv_kk[j], rwkv_ka[j],
                  rwkv_rk[j], rwkv_lnx_g[j], rwkv_lnx_b[j])
            s_zero = jnp.zeros((2, Bp // SCAN_BATCH, RWKV_HEAD, RWKV_HEAD, LANES), F32)
            xp, hp, s_ctx = rwkv_sublayer(xp, mod_p, s_zero, ln_g[i, 0], ln_b[i, 0], *rw)
            xs, hs, _ = rwkv_sublayer(xs, mod_s, state_to_chains(state_rwkv[:, j]), ln_g[i, 0], ln_b[i, 0], *rw)
            states.append(state_from_chains(s_ctx))
        else:
            op, kp, vp = attention_context(hp, attn_wqkv[j], attn_wo[j], attn_qn[j], attn_kn[j])
            os_ = attention_latent(hs, cache_k[:, j], cache_v[:, j], attn_wqkv[j], attn_wo[j], attn_qn[j], attn_kn[j])
            keys_out.append(kp)
            vals_out.append(vp)
            xp, hp = ln_mod(xp, op, mod_p[:, 2:3], mod_p[:, 3:4], mod_p[:, 4:5], ln_g[i, 0], ln_b[i, 0])
            xs, hs = ln_mod(xs, os_, mod_s[:, 2:3], mod_s[:, 3:4], mod_s[:, 4:5], ln_g[i, 0], ln_b[i, 0])
        h_all = jnp.concatenate([hp.reshape(n_p, D), hs.reshape(-1, D)], axis=0)
        f_all = peer_ffn(h_all, peer_wq[i], peer_keys[i], peer_u[i].T.astype(BF16), peer_v[i].astype(BF16))
        fp = f_all[:n_p].reshape(xp.shape)
        fs = f_all[n_p:].reshape(xs.shape)
        nxt_p, nxt_s = mods[min(i + 1, DEPTH - 1)]
        xp, hp = ln_mod(xp, fp, mod_p[:, 5:6], nxt_p[:, 0:1], nxt_p[:, 1:2], ln_g[i, 1], ln_b[i, 1])
        xs, hs = ln_mod(xs, fs, mod_s[:, 5:6], nxt_s[:, 0:1], nxt_s[:, 1:2], ln_g[i, 1], ln_b[i, 1])
    return (xp, xs, jnp.stack(states, axis=1), jnp.stack(keys_out, axis=1), jnp.stack(vals_out, axis=1))
```

```python
import functools

import numpy as np
import jax
import jax.numpy as jnp
from jax import lax
from jax.experimental import pallas as pl
from jax.experimental.pallas import tpu as pltpu

D_MODEL = 1024
DEPTH = 4
GRID_W = 64
ALPHA = (2 * DEPTH) ** 0.25
LN_EPS = 1e-5
RWKV_HEAD = 64
RWKV_HEADS = D_MODEL // RWKV_HEAD
DECAY_LORA = 64
GN_EPS = RWKV_HEAD * 1e-5
HEAD_DIM = 64
N_HEADS = D_MODEL // HEAD_DIM
KV_HEADS = 4
GROUP = N_HEADS // KV_HEADS
Q_WIDTH = N_HEADS * HEAD_DIM
KV_WIDTH = KV_HEADS * HEAD_DIM
ROPE_THETA = 10000.0
ROPE_FREQS = HEAD_DIM // 4
ATTN_SCALE = HEAD_DIM ** -0.5
RMS_EPS = 1e-6
PEER_HEADS = 8
N_KEYS = 128
PEER_QUERY = 256
PEER_HALF = PEER_QUERY // 2
PEER_TOPK = 16

LANES = 128
MIB = 1024 * 1024
BF16 = jnp.bfloat16
F32 = jnp.float32


def _mm_kernel(x_ref, w_ref, o_ref):
    o_ref[...] = jnp.dot(x_ref[...].astype(BF16), w_ref[...].astype(BF16), preferred_element_type=F32)


def matmul(x, w, tm=512, tn=512):
    M, K = x.shape
    N = w.shape[1]
    tm = min(tm, M)
    tn = min(tn, N)
    assert M % tm == 0 and N % tn == 0, (M, N)
    return pl.pallas_call(
        _mm_kernel,
        grid=(M // tm, N // tn),
        in_specs=[pl.BlockSpec((tm, K), lambda i, j: (i, 0)),
                  pl.BlockSpec((K, tn), lambda i, j: (0, j))],
        out_specs=pl.BlockSpec((tm, tn), lambda i, j: (i, j)),
        out_shape=jax.ShapeDtypeStruct((M, N), F32),
        compiler_params=pltpu.CompilerParams(dimension_semantics=("parallel", "parallel")),
        name="matmul",
    )(x, w)


def mm3(x, w):
    return matmul(x.reshape(-1, x.shape[-1]), w).reshape(x.shape[:-1] + (w.shape[1],))


def _bdot(a, b):
    return jnp.dot(a.astype(BF16), b, preferred_element_type=F32)


SEG_BLOCK = 256


def _seg_ones():
    i = np.arange(SEG_BLOCK) // RWKV_HEAD
    return jnp.asarray((i[:, None] == i[None, :]).astype(np.float32), dtype=BF16)


def _segsum(q, seg):
    hi = q.astype(BF16)
    lo = (q - hi.astype(F32)).astype(BF16)
    return jnp.concatenate(
        [jnp.dot(hi[:, b:b + SEG_BLOCK], seg, preferred_element_type=F32)
         + jnp.dot(lo[:, b:b + SEG_BLOCK], seg, preferred_element_type=F32)
         for b in range(0, q.shape[1], SEG_BLOCK)], axis=1)


HEAD_PAIRS = D_MODEL // LANES
ROW_PAD = 8


def _store_head_rows(out_ref, val, slab_ref, tm):
    pitch = tm + ROW_PAD
    for p in range(HEAD_PAIRS):
        slab_ref[pl.ds(p * pitch, tm), :] = val[:, p * LANES:(p + 1) * LANES]

    def body(t, carry):
        out_ref[pl.ds(pl.multiple_of(t * HEAD_PAIRS, HEAD_PAIRS), HEAD_PAIRS), :] = (
            slab_ref[pl.ds(t, HEAD_PAIRS, stride=pitch), :])
        return carry

    lax.fori_loop(0, tm, body, 0, unroll=8)


def _load_head_rows(row_fn, slab_ref, tm):
    pitch = tm + ROW_PAD

    def body(t, carry):
        slab_ref[pl.ds(t, HEAD_PAIRS, stride=pitch), :] = row_fn(t)
        return carry

    lax.fori_loop(0, tm, body, 0, unroll=8)
    return jnp.concatenate([slab_ref[pl.ds(p * pitch, tm), :] for p in range(HEAD_PAIRS)], axis=1)


def _layer_norm_mod(z, g, b, shift, scale):
    mu = jnp.mean(z, -1, keepdims=True)
    zc = z - mu
    var = jnp.mean(zc * zc, -1, keepdims=True)
    xn = zc * lax.rsqrt(var + LN_EPS) * g + b
    return xn, xn * (1.0 + scale) + shift


def _ln_mod_kernel(x_ref, f_ref, gate_ref, shift_ref, scale_ref, vec_ref, xn_ref, h_ref):
    z = ALPHA * x_ref[...] + gate_ref[...] * f_ref[...]
    xn, h = _layer_norm_mod(z, vec_ref[0:1, :], vec_ref[1:2, :], shift_ref[...], scale_ref[...])
    xn_ref[...] = xn
    h_ref[...] = h


def ln_mod(x, f, gate, shift, scale, ln_g, ln_b, *, tm=256):
    B, T, D = x.shape
    vec = jnp.concatenate([ln_g[None], ln_b[None], jnp.zeros((6, D), F32)], axis=0)
    tok = pl.BlockSpec((None, tm, D), lambda b, i: (b, i, 0))
    per_b = pl.BlockSpec((None, 1, D), lambda b, i: (b, 0, 0))
    o1 = jax.ShapeDtypeStruct((B, T, D), F32)
    return pl.pallas_call(
        _ln_mod_kernel,
        grid=(B, T // tm),
        in_specs=[tok, tok, per_b, per_b, per_b, pl.BlockSpec((8, D), lambda b, i: (0, 0))],
        out_specs=[tok, tok],
        out_shape=[o1, o1],
        compiler_params=pltpu.CompilerParams(dimension_semantics=("parallel", "parallel")),
        name="ln_mod",
    )(x, f, gate, shift, scale, vec)


def _softplus(x):
    return jnp.maximum(x, 0.0) + jnp.log1p(jnp.exp(-jnp.abs(x)))


def _rwkv_pre_kernel(x_ref, xp_ref, xn_ref, shift_ref, scale_ref, mu_ref, vec_ref, wr_ref, wk_ref, wv_ref,
                     l1_ref, w2_ref, a2_ref, g2_ref, seg_ref,
                     r_ref, v_ref, kk_ref, w_ref, kd_ref, ka_ref, g_ref, bonus_ref, slab_ref, *, tm):
    i = pl.program_id(1)
    last = pl.num_programs(1) - 1
    one_scale = 1.0 + scale_ref[...]
    shift = shift_ref[...]
    h = x_ref[...] * one_scale + shift
    h_before = jnp.where(i == 0, 0.0, xp_ref[7:8, :] * one_scale + shift)
    h_after = jnp.where(i == last, 0.0, xn_ref[0:1, :] * one_scale + shift)
    row = lax.broadcasted_iota(jnp.int32, (tm, 1), 0)
    prev = jnp.where(row == 0, h_before, pltpu.roll(h, 1, axis=0))
    nxt = jnp.where(row == tm - 1, h_after, pltpu.roll(h, tm - 1, axis=0))
    xx = 0.5 * (prev + nxt) - h
    mix = lambda m: (h + xx * mu_ref[m:m + 1, :]).astype(BF16)
    w0, a0 = vec_ref[0:2, :], vec_ref[2:4, :]
    k_k, k_a, r_k = vec_ref[4:5, :], vec_ref[5:6, :], vec_ref[6:7, :]
    seg = seg_ref[...]
    r = jnp.dot(mix(0), wr_ref[...], preferred_element_type=F32)
    k = jnp.dot(mix(2), wk_ref[...], preferred_element_type=F32)
    v = jnp.dot(mix(3), wv_ref[...], preferred_element_type=F32)
    lw = jnp.tanh(jnp.dot(mix(1), l1_ref[0], preferred_element_type=F32))
    la = jnp.dot(mix(4), l1_ref[1], preferred_element_type=F32)
    lg = jax.nn.sigmoid(jnp.dot(mix(5), l1_ref[2], preferred_element_type=F32))
    g_ref[...] = _bdot(lg, g2_ref[...])
    kkr = k * k_k
    kk = kkr / jnp.maximum(jnp.sqrt(_segsum(kkr * kkr, seg)), 1e-12)
    _store_head_rows(r_ref, r, slab_ref, tm)
    _store_head_rows(v_ref, v, slab_ref, tm)
    _store_head_rows(kk_ref, kk, slab_ref, tm)
    kd_sum = jnp.zeros_like(k)
    for z in range(2):
        half = slice(z * DECAY_LORA, (z + 1) * DECAY_LORA)
        wl = w0[z:z + 1, :] + _bdot(lw[:, half], w2_ref[z])
        _store_head_rows(w_ref.at[z], jnp.exp(-jnp.exp(-_softplus(-wl) - 0.5)), slab_ref, tm)
        a = jax.nn.sigmoid(a0[z:z + 1, :] + _bdot(la[:, half], a2_ref[z]))
        kd = k * (1.0 + (a - 1.0) * k_a)
        _store_head_rows(kd_ref.at[z], kd, slab_ref, tm)
        _store_head_rows(ka_ref.at[z], kk * a, slab_ref, tm)
        kd_sum = kd_sum + r * kd * r_k
    bonus_ref[...] = _segsum(kd_sum, seg) * v


def rwkv_pre(x, shift, scale, mu, wrkv, w0, w1, w2, a0, a1, a2, g1, g2, k_k, k_a, r_k, *, tm=128):
    B, T, D = x.shape
    nb = tm // 8
    vec = jnp.concatenate([w0, a0, k_k[None], k_a[None], r_k.reshape(1, D), jnp.zeros((1, D), F32)], axis=0)
    l1 = jnp.stack([jnp.concatenate([w1[0], w1[1]], axis=1), jnp.concatenate([a1[0], a1[1]], axis=1), g1]).astype(BF16)
    tok = pl.BlockSpec((None, tm, D), lambda b, i: (b, i, 0))
    tok2 = pl.BlockSpec((2, None, tm, D), lambda b, i: (0, b, i, 0))
    whole = lambda a: pl.BlockSpec(a.shape, lambda b, i: (0,) * a.ndim)
    per_b = pl.BlockSpec((None, 1, D), lambda b, i: (b, 0, 0))
    consts = [mu, vec, wrkv[0].astype(BF16), wrkv[1].astype(BF16), wrkv[2].astype(BF16), l1,
              w2.astype(BF16), a2.astype(BF16), g2.astype(BF16), _seg_ones()]
    o1 = jax.ShapeDtypeStruct((B, T, D), F32)
    r1 = jax.ShapeDtypeStruct((B, T * HEAD_PAIRS, LANES), F32)
    r2 = jax.ShapeDtypeStruct((2, B, T * HEAD_PAIRS, LANES), F32)
    rows = pl.BlockSpec((None, tm * HEAD_PAIRS, LANES), lambda b, i: (b, i, 0))
    rows2 = pl.BlockSpec((2, None, tm * HEAD_PAIRS, LANES), lambda b, i: (0, b, i, 0))
    return pl.pallas_call(
        functools.partial(_rwkv_pre_kernel, tm=tm),
        grid=(B, T // tm),
        in_specs=[tok,
                  pl.BlockSpec((None, 8, D), lambda b, i: (b, jnp.maximum(i * nb - 1, 0), 0)),
                  pl.BlockSpec((None, 8, D), lambda b, i: (b, jnp.minimum((i + 1) * nb, T // 8 - 1), 0)),
                  per_b, per_b] + [whole(a) for a in consts],
        out_specs=[rows, rows, rows, rows2, rows2, rows2, tok, tok],
        out_shape=[r1, r1, r1, r2, r2, r2, o1, o1],
        scratch_shapes=[pltpu.VMEM((HEAD_PAIRS * (tm + ROW_PAD), LANES), F32)],
        compiler_params=pltpu.CompilerParams(dimension_semantics=("parallel", "parallel"),
                                             vmem_limit_bytes=48 * MIB),
        name="rwkv_pre",
    )(x, x, x, shift, scale, *consts)


def _rwkv_post_kernel(y_ref, bonus_ref, g_ref, x_ref, gate_ref, shift_ref, scale_ref, vec_ref, wo_ref, seg_ref,
                      xn_ref, h_ref, slab_ref, *, tm):
    seg = seg_ref[...]

    def y_rows(t):
        rows = pl.ds(pl.multiple_of(t * HEAD_PAIRS, HEAD_PAIRS), HEAD_PAIRS)
        return y_ref[0, rows, :] + y_ref[1, rows, :]

    y = _load_head_rows(y_rows, slab_ref, tm)
    yc = y - _segsum(y, seg) * (1.0 / RWKV_HEAD)
    yv = _segsum(yc * yc, seg) * (1.0 / RWKV_HEAD)
    yn = yc * lax.rsqrt(yv + GN_EPS) * vec_ref[0:1, :] + vec_ref[1:2, :]
    o = _bdot((yn + bonus_ref[...]) * g_ref[...], wo_ref[...])
    z = ALPHA * x_ref[...] + gate_ref[...] * o
    xn, h = _layer_norm_mod(z, vec_ref[2:3, :], vec_ref[3:4, :], shift_ref[...], scale_ref[...])
    xn_ref[...] = xn
    h_ref[...] = h


def rwkv_post(y, bonus, g, x, gate, shift, scale, lnx_g, lnx_b, wo, ln_g, ln_b, *, tm=256):
    B, T, D = x.shape
    vec = jnp.concatenate([lnx_g[None], lnx_b[None], ln_g[None], ln_b[None], jnp.zeros((4, D), F32)], axis=0)
    tok = pl.BlockSpec((None, tm, D), lambda b, i: (b, i, 0))
    per_b = pl.BlockSpec((None, 1, D), lambda b, i: (b, 0, 0))
    whole = lambda a: pl.BlockSpec(a.shape, lambda b, i: (0,) * a.ndim)
    consts = [vec, wo.astype(BF16), _seg_ones()]
    o1 = jax.ShapeDtypeStruct((B, T, D), F32)
    return pl.pallas_call(
        functools.partial(_rwkv_post_kernel, tm=tm),
        grid=(B, T // tm),
        in_specs=[pl.BlockSpec((2, None, tm * HEAD_PAIRS, LANES), lambda b, i: (0, b, i, 0)),
                  tok, tok, tok, per_b, per_b, per_b] + [whole(a) for a in consts],
        out_specs=[tok, tok],
        out_shape=[o1, o1],
        scratch_shapes=[pltpu.VMEM((HEAD_PAIRS * (tm + ROW_PAD), LANES), F32)],
        compiler_params=pltpu.CompilerParams(dimension_semantics=("parallel", "parallel")),
        name="rwkv_post",
    )(y, bonus, g, x, gate, shift, scale, *consts)


SCAN_ROWS = 32
SCAN_CHAINS = 4
SCAN_BATCH = 8


def _tree_sum(parts):
    while len(parts) > 1:
        parts = [parts[i] + parts[i + 1] for i in range(0, len(parts), 2)]
    return parts[0]


def _to_chain_tiles(src_ref, t0):
    z = jnp.concatenate([src_ref[:, pl.ds((t0 + s) * HEAD_PAIRS, HEAD_PAIRS), :].reshape(SCAN_BATCH * HEAD_PAIRS, LANES)
                         for s in range(2)], axis=0)
    zt = z.T
    top, bot = zt[:RWKV_HEAD], zt[RWKV_HEAD:]
    low = lax.broadcasted_iota(jnp.int32, (RWKV_HEAD, LANES), 1) < RWKV_HEAD
    return (jnp.where(low, top, pltpu.roll(bot, RWKV_HEAD, axis=1)),
            jnp.where(low, pltpu.roll(top, RWKV_HEAD, axis=1), bot))


def _from_chain_tiles(y0, y1):
    low = lax.broadcasted_iota(jnp.int32, (RWKV_HEAD, LANES), 1) < RWKV_HEAD
    top = jnp.where(low, y0, pltpu.roll(y1, RWKV_HEAD, axis=1))
    bot = jnp.where(low, pltpu.roll(y0, RWKV_HEAD, axis=1), y1)
    return jnp.concatenate([top, bot], axis=0).T


def _wkv_scan_kernel(r_ref, w_ref, k_ref, v_ref, kk_ref, ka_ref, s0_ref, y_ref, sf_ref,
                     st_ref, rs, ws, ks_, vs, kks, kas, *, tc):
    d = pl.program_id(0)
    c = pl.program_id(2)
    n = RWKV_HEAD
    npairs = tc // 2
    srcs = (r_ref, w_ref, k_ref, v_ref, kk_ref, ka_ref)
    dsts = (rs, ws, ks_, vs, kks, kas)
    fwd = d == 0

    @pl.when(c == 0)
    def _init():
        st_ref[...] = s0_ref[...]

    def pair_start(p):
        return 2 * jnp.where(fwd, p, npairs - 1 - p)

    def stage(p, slot):
        t0 = pair_start(jnp.minimum(p, npairs - 1))
        for src, dst in zip(srcs, dsts):
            a, b = _to_chain_tiles(src, t0)
            dst[2 * slot] = jnp.where(fwd, a, b)
            dst[2 * slot + 1] = jnp.where(fwd, b, a)

    def one_step(i):
        halves = []
        for r0 in range(0, n, SCAN_ROWS):
            rows = pl.ds(r0, SCAN_ROWS)
            parts = [st_ref[j, rows, :] * kks[i, pl.ds(j, 1), :] for j in range(SCAN_CHAINS)]
            for j in range(SCAN_CHAINS, n):
                parts[j % SCAN_CHAINS] = parts[j % SCAN_CHAINS] + st_ref[j, rows, :] * kks[i, pl.ds(j, 1), :]
            skk = _tree_sum(parts)
            v_t = vs[i, rows, :]
            parts = []
            for j in range(n):
                s = (st_ref[j, rows, :] * ws[i, pl.ds(j, 1), :] - skk * kas[i, pl.ds(j, 1), :]
                     + v_t * ks_[i, pl.ds(j, 1), :])
                st_ref[j, rows, :] = s
                p = s * rs[i, pl.ds(j, 1), :]
                if j < SCAN_CHAINS:
                    parts.append(p)
                else:
                    parts[j % SCAN_CHAINS] = parts[j % SCAN_CHAINS] + p
            halves.append(_tree_sum(parts))
        return jnp.concatenate(halves, axis=0)

    def run_pair(p, slot):
        stage(p + 1, 1 - slot)
        y_a = one_step(2 * slot)
        y_b = one_step(2 * slot + 1)
        out = _from_chain_tiles(jnp.where(fwd, y_a, y_b), jnp.where(fwd, y_b, y_a))
        t0 = pair_start(p)
        for s in range(2):
            y_ref[:, pl.ds((t0 + s) * HEAD_PAIRS, HEAD_PAIRS), :] = out[s * 64:(s + 1) * 64].reshape(
                SCAN_BATCH, HEAD_PAIRS, LANES)

    stage(0, 0)

    def two_pairs(q, carry):
        run_pair(2 * q, 0)
        run_pair(2 * q + 1, 1)
        return carry

    lax.fori_loop(0, npairs // 2, two_pairs, 0)

    @pl.when(c == pl.num_programs(2) - 1)
    def _fin():
        sf_ref[...] = st_ref[...]


def wkv_scan_tokens(r, w, k, v, kk, ka, s0, *, tc=16):
    B = r.shape[0]
    T = r.shape[1] // HEAD_PAIRS
    n = RWKV_HEAD
    G = B // SCAN_BATCH
    assert r.shape[2] == LANES and B % SCAN_BATCH == 0 and T % tc == 0 and tc % 4 == 0
    nc = T // tc
    cm = lambda d, c: jnp.where(d == 0, c, nc - 1 - c)
    shared = pl.BlockSpec((SCAN_BATCH, tc * HEAD_PAIRS, LANES), lambda d, g, c: (g, cm(d, c), 0))
    perdir = pl.BlockSpec((None, SCAN_BATCH, tc * HEAD_PAIRS, LANES), lambda d, g, c: (d, g, cm(d, c), 0))
    state = pl.BlockSpec((None, None, n, n, LANES), lambda d, g, c: (d, g, 0, 0, 0))
    tile = pltpu.VMEM((4, n, LANES), F32)
    return pl.pallas_call(
        functools.partial(_wkv_scan_kernel, tc=tc),
        grid=(2, G, nc),
        in_specs=[shared, perdir, perdir, shared, shared, perdir, state],
        out_specs=[perdir, state],
        out_shape=[jax.ShapeDtypeStruct((2, B, T * HEAD_PAIRS, LANES), F32),
                   jax.ShapeDtypeStruct((2, G, n, n, LANES), F32)],
        scratch_shapes=[pltpu.VMEM((n, n, LANES), F32)] + [tile] * 6,
        compiler_params=pltpu.CompilerParams(dimension_semantics=("parallel", "parallel", "arbitrary")),
        name="wkv_scan",
    )(r, w, k, v, kk, ka, s0)


def state_to_chains(s):
    B = s.shape[0]
    s = s.reshape(B // SCAN_BATCH, SCAN_BATCH, 2, HEAD_PAIRS, 2, RWKV_HEAD, RWKV_HEAD)
    return s.transpose(2, 0, 6, 5, 4, 1, 3).reshape(2, B // SCAN_BATCH, RWKV_HEAD, RWKV_HEAD, LANES)


def state_from_chains(s):
    G = s.shape[1]
    s = s.reshape(2, G, RWKV_HEAD, RWKV_HEAD, 2, SCAN_BATCH, HEAD_PAIRS)
    return s.transpose(1, 5, 0, 6, 4, 3, 2).reshape(G * SCAN_BATCH, 2, 2 * HEAD_PAIRS, RWKV_HEAD, RWKV_HEAD)


def rwkv_sublayer(x, mod, s0_chains, ln_g, ln_b, mu, wrkv, wo, w0, w1, w2, a0, a1, a2, g1, g2, k_k, k_a, r_k,
                  lnx_g, lnx_b):
    r, v, kk, w, kd, ka, g, bonus = rwkv_pre(x, mod[:, 0:1], mod[:, 1:2], mu, wrkv, w0, w1, w2, a0, a1, a2,
                                             g1, g2, k_k, k_a, r_k)
    y, s_fin = wkv_scan_tokens(r, w, kd, v, kk, ka, s0_chains)
    xn, h = rwkv_post(y, bonus, g, x, mod[:, 2:3], mod[:, 3:4], mod[:, 4:5], lnx_g, lnx_b, wo, ln_g, ln_b)
    return xn, h, s_fin


def rms_norm(x, g):
    return x * lax.rsqrt(jnp.mean(x * x, -1, keepdims=True) + RMS_EPS) * g


def axial_rope_angles(n):
    rows = n // GRID_W
    row = jnp.repeat(jnp.arange(rows), GRID_W).astype(F32)
    col = (jnp.arange(rows * GRID_W) % GRID_W).astype(F32)
    freqs = ROPE_THETA ** (-jnp.arange(ROPE_FREQS, dtype=F32) / ROPE_FREQS)
    ang = jnp.stack([row[:, None] * freqs, col[:, None] * freqs], axis=1)
    return jnp.cos(ang), jnp.sin(ang)


def apply_axial_rope(x, cos, sin):
    xs = x.reshape(x.shape[:-1] + (2, 2, ROPE_FREQS))
    bshape = (x.shape[1],) + (1,) * (x.ndim - 3) + (2, ROPE_FREQS)
    cos, sin = cos.reshape(bshape), sin.reshape(bshape)
    x1, x2 = xs[..., 0, :], xs[..., 1, :]
    out = jnp.stack([x1 * cos - x2 * sin, x2 * cos + x1 * sin], axis=-2)
    return out.reshape(x.shape)


def attn_project(h, wqkv, qn, kn):
    B, n, _ = h.shape
    qkv = mm3(h, wqkv)
    q = rms_norm(qkv[..., :Q_WIDTH].reshape(B, n, KV_HEADS, GROUP, HEAD_DIM), qn)
    k = rms_norm(qkv[..., Q_WIDTH:Q_WIDTH + KV_WIDTH].reshape(B, n, KV_HEADS, HEAD_DIM), kn)
    v = qkv[..., Q_WIDTH + KV_WIDTH:].reshape(B, n, KV_HEADS, HEAD_DIM)
    return q, k, v


def _attn_kernel(q_ref, kt_ref, v_ref, o_ref):
    kt = kt_ref[...].astype(BF16)
    v = v_ref[...].astype(BF16)
    outs = []
    for g in range(GROUP):
        qg = q_ref[:, g * HEAD_DIM:(g + 1) * HEAD_DIM].astype(BF16)
        s = jnp.dot(qg, kt, preferred_element_type=F32) * ATTN_SCALE
        e = jnp.exp(s - jnp.max(s, axis=-1, keepdims=True))
        l = jnp.sum(e, axis=-1, keepdims=True)
        outs.append(jnp.dot(e.astype(BF16), v, preferred_element_type=F32) / l)
    o_ref[...] = jnp.concatenate(outs, axis=-1)


def attention(q, k, v, *, tq=256):
    B, n = q.shape[:2]
    S = k.shape[1]
    gw = GROUP * HEAD_DIM
    return pl.pallas_call(
        _attn_kernel,
        grid=(B, KV_HEADS, n // tq),
        in_specs=[pl.BlockSpec((None, tq, gw), lambda b, h, i: (b, i, h)),
                  pl.BlockSpec((None, None, HEAD_DIM, S), lambda b, h, i: (b, h, 0, 0)),
                  pl.BlockSpec((None, None, S, HEAD_DIM), lambda b, h, i: (b, h, 0, 0))],
        out_specs=pl.BlockSpec((None, tq, gw), lambda b, h, i: (b, i, h)),
        out_shape=jax.ShapeDtypeStruct((B, n, Q_WIDTH), F32),
        compiler_params=pltpu.CompilerParams(dimension_semantics=("parallel", "parallel", "parallel")),
        name="attention",
    )(q.reshape(B, n, Q_WIDTH), k.transpose(0, 2, 3, 1), v.transpose(0, 2, 1, 3))


def attention_context(h, wqkv, wo, qn, kn):
    q, k, v = attn_project(h, wqkv, qn, kn)
    return mm3(attention(q, k, v), wo), k, v


def attention_latent(h, ctx_k, ctx_v, wqkv, wo, qn, kn):
    q, k, v = attn_project(h, wqkv, qn, kn)
    cos, sin = axial_rope_angles(h.shape[1])
    q = apply_axial_rope(q, cos, sin)
    k = apply_axial_rope(k, cos, sin)
    keys = jnp.concatenate([ctx_k, k], axis=1)
    vals = jnp.concatenate([ctx_v, v], axis=1)
    return mm3(attention(q, keys, vals), wo)


CUBE_PITCH = 136
_HI_MASK = -65536


def _gelu(x):
    return 0.5 * x * (1.0 + lax.erf(x * (2.0 ** -0.5)))


def _gate_tile(a_row, b_row, g_row, row_id):
    at = jnp.where(row_id == a_row, g_row, 0.0).astype(BF16)
    bt = jnp.where(row_id == b_row, 1.0, 0.0).astype(BF16)
    return lax.dot_general(at, bt, (((1,), (1,)), ((), ())), preferred_element_type=F32)


def _bf16_bits(x):
    return lax.bitcast_convert_type(x.astype(BF16).astype(F32), jnp.int32)


def _peer_experts_kernel(x_ref, a_ref, b_ref, g_ref, ut_ref, v_ref, o_ref, xb_ref, cube_ref, *, tm, ib):
    e = pl.program_id(1)
    half = tm // 2

    @pl.when(e == 0)
    def _build():
        xb_ref[...] = x_ref[...].astype(BF16)
        o_ref[...] = jnp.zeros_like(o_ref)
        row_id = lax.broadcasted_iota(jnp.int32, (N_KEYS, N_KEYS), 0)

        def per_pair(t, carry):
            t2 = t + half
            g_hi = _gate_tile(a_ref[pl.ds(t, 1), :], b_ref[pl.ds(t, 1), :], g_ref[pl.ds(t, 1), :], row_id)
            g_lo = _gate_tile(a_ref[pl.ds(t2, 1), :], b_ref[pl.ds(t2, 1), :], g_ref[pl.ds(t2, 1), :], row_id)
            word = _bf16_bits(g_hi) | lax.shift_right_logical(_bf16_bits(g_lo), 16)
            cube_ref[pl.ds(pl.multiple_of(t * CUBE_PITCH, 8), N_KEYS), :] = word
            return carry

        lax.fori_loop(0, half, per_pair, 0, unroll=16)

    s = jnp.dot(xb_ref[...], ut_ref[...], preferred_element_type=F32)
    words = jnp.concatenate([cube_ref[pl.ds(e * ib + p, half, stride=CUBE_PITCH), :] for p in range(ib)], axis=1)
    gates = jnp.concatenate([lax.bitcast_convert_type(words & _HI_MASK, F32),
                             lax.bitcast_convert_type(lax.shift_left(words, 16), F32)], axis=0)
    h = (gates * _gelu(s)).astype(BF16)
    o_ref[...] += jnp.dot(h, v_ref[...], preferred_element_type=F32)


def peer_experts(x, a_idx, b_idx, gate, ut_bf16, v_bf16, *, tm=512, ib=16):
    T, D = x.shape
    P = a_idx.shape[1]
    assert P == N_KEYS and T % tm == 0 and N_KEYS % ib == 0
    kern = functools.partial(_peer_experts_kernel, tm=tm, ib=ib)
    return pl.pallas_call(
        kern,
        grid=(T // tm, N_KEYS // ib),
        in_specs=[pl.BlockSpec((tm, D), lambda i, e: (i, 0)),
                  pl.BlockSpec((tm, P), lambda i, e: (i, 0)),
                  pl.BlockSpec((tm, P), lambda i, e: (i, 0)),
                  pl.BlockSpec((tm, P), lambda i, e: (i, 0)),
                  pl.BlockSpec((D, ib * N_KEYS), lambda i, e: (0, e)),
                  pl.BlockSpec((ib * N_KEYS, D), lambda i, e: (e, 0))],
        out_specs=pl.BlockSpec((tm, D), lambda i, e: (i, 0)),
        out_shape=jax.ShapeDtypeStruct((T, D), F32),
        scratch_shapes=[pltpu.VMEM((tm, D), BF16),
                        pltpu.VMEM((tm // 2 * CUBE_PITCH, N_KEYS), jnp.int32)],
        compiler_params=pltpu.CompilerParams(dimension_semantics=("parallel", "arbitrary"),
                                             vmem_limit_bytes=52 * MIB),
        name="peer_experts",
    )(x, a_idx, b_idx, gate, ut_bf16, v_bf16)


_CAND_COLS = (16, 8, 8, 8, 8, 8, 8, 8)
_CAND_ROWS = 80


def _cand_tables():
    idx = np.zeros((_CAND_ROWS, 1), np.int32)
    valid = np.zeros((_CAND_ROWS, 1), np.float32)
    pos = 0
    for r, n in enumerate(_CAND_COLS):
        for cc in range(n):
            idx[pos] = r * PEER_TOPK + cc
            valid[pos] = 1.0 if (r + 1) * (cc + 1) <= PEER_TOPK else 0.0
            pos += 1
    for r in range(8, 16):
        idx[pos] = r * PEER_TOPK
        valid[pos] = 1.0
        pos += 1
    assert pos == _CAND_ROWS
    return (jnp.asarray(np.broadcast_to(idx, (_CAND_ROWS, LANES)).copy()),
            jnp.asarray(np.broadcast_to(valid, (_CAND_ROWS, LANES)).copy()))


def _top_rows(s, ids, k, sentinel):
    vals, picks = [], []
    for _ in range(k):
        m = jnp.max(s, axis=0, keepdims=True)
        pick = jnp.min(jnp.where(s == m, ids, sentinel), axis=0, keepdims=True)
        vals.append(m)
        picks.append(pick)
        s = jnp.where(ids == pick, -jnp.inf, s)
    return jnp.concatenate(vals, axis=0), jnp.concatenate(picks, axis=0)


def _take_rows(table, rows):
    out = jnp.zeros(rows.shape, table.dtype)
    for rr in range(table.shape[0]):
        out = jnp.where(rows == rr, table[rr:rr + 1, :], out)
    return out


def _peer_route_kernel(x_ref, wqt_ref, keys_ref, cidx_ref, cvalid_ref, a_ref, b_ref, g_ref, q_ref, *, tm):
    xb = x_ref[...].astype(BF16)
    q_ref[...] = lax.dot_general(wqt_ref[...], xb, (((1,), (1,)), ((), ())),
                                 preferred_element_type=F32).astype(BF16)
    key_ids = lax.broadcasted_iota(jnp.int32, (N_KEYS, LANES), 0)

    def where(it):
        return it % PEER_HEADS, pl.multiple_of((it // PEER_HEADS) * LANES, LANES)

    def sub_key_tops(it):
        h, lane0 = where(it)
        out = []
        for z in range(2):
            row0 = pl.multiple_of((h * 2 + z) * PEER_HALF, PEER_HALF)
            qhz = q_ref[pl.ds(row0, PEER_HALF), pl.ds(lane0, LANES)]
            s = jnp.dot(keys_ref[z], qhz, preferred_element_type=F32)
            out.extend(_top_rows(s, key_ids, PEER_TOPK, N_KEYS))
        return tuple(out)

    def pick_experts(it, tops):
        h, lane0 = where(it)
        sv0, si0, sv1, si1 = tops
        pieces = [sv0[r:r + 1, :] + sv1[:n, :] for r, n in enumerate(_CAND_COLS)]
        pieces.append(sv0[8:16, :] + sv1[0:1, :])
        cand = jnp.where(cvalid_ref[...] > 0.0, jnp.concatenate(pieces, axis=0), -jnp.inf)
        cv, ci = _top_rows(cand, cidx_ref[...], PEER_TOPK, PEER_TOPK * PEER_TOPK)
        i1 = _take_rows(si0, ci >> 4)
        i2 = _take_rows(si1, ci & (PEER_TOPK - 1))
        e = jnp.exp(cv - jnp.max(cv, axis=0, keepdims=True))
        gate = e / jnp.sum(e, axis=0, keepdims=True)
        rows = pl.ds(pl.multiple_of(h * PEER_TOPK, PEER_TOPK), PEER_TOPK)
        a_ref[rows, pl.ds(lane0, LANES)] = i1
        b_ref[rows, pl.ds(lane0, LANES)] = i2
        g_ref[rows, pl.ds(lane0, LANES)] = gate

    def body(it, tops):
        pick_experts(it - 1, tops)
        return sub_key_tops(it)

    n_items = PEER_HEADS * (tm // LANES)
    pick_experts(n_items - 1, lax.fori_loop(1, n_items, body, sub_key_tops(0)))


def peer_route(x, wqt_bf16, keys_bf16, *, tm=256):
    T, D = x.shape
    QW = wqt_bf16.shape[0]
    P = PEER_HEADS * PEER_TOPK
    cidx, cvalid = _cand_tables()
    whole = lambda shape: pl.BlockSpec(shape, lambda i: (0,) * len(shape))
    out = pl.BlockSpec((P, tm), lambda i: (0, i))
    return pl.pallas_call(
        functools.partial(_peer_route_kernel, tm=tm),
        grid=(T // tm,),
        in_specs=[pl.BlockSpec((tm, D), lambda i: (i, 0)), whole((QW, D)), whole((2, N_KEYS, PEER_HALF)),
                  whole((_CAND_ROWS, LANES)), whole((_CAND_ROWS, LANES))],
        out_specs=[out, out, out],
        out_shape=[jax.ShapeDtypeStruct((P, T), jnp.int32), jax.ShapeDtypeStruct((P, T), jnp.int32),
                   jax.ShapeDtypeStruct((P, T), F32)],
        scratch_shapes=[pltpu.VMEM((QW, tm), BF16)],
        compiler_params=pltpu.CompilerParams(dimension_semantics=("parallel",)),
        name="peer_route",
    )(x, wqt_bf16, keys_bf16, cidx, cvalid)


def peer_ffn(x, wq, sub_keys, ut_bf16, v_bf16):
    a, b, g = peer_route(x, wq.T.astype(BF16), sub_keys.astype(BF16))
    return peer_experts(x, a.T, b.T, g.T, ut_bf16, v_bf16)


def kernel(x_prompt, x_sample, c, state_rwkv, cache_k, cache_v, c_ctx, ada_w, ada_b, ln_g, ln_b,
           rwkv_mu, rwkv_wrkv, rwkv_wo, rwkv_w0, rwkv_w1, rwkv_w2, rwkv_a0, rwkv_a1, rwkv_a2,
           rwkv_g1, rwkv_g2, rwkv_kk, rwkv_ka, rwkv_rk, rwkv_lnx_g, rwkv_lnx_b,
           attn_wqkv, attn_wo, attn_qn, attn_kn, peer_wq, peer_keys, peer_u, peer_v):
    Bp, Tp, D = x_prompt.shape
    Bs, Ts, _ = x_sample.shape
    n_p = Bp * Tp
    cond = jax.nn.silu(jnp.concatenate([c_ctx[None], c], axis=0))
    mods = []
    for i in range(DEPTH):
        m = (matmul(cond, ada_w[i]) + ada_b[i]).reshape(1 + Bs, 6, D)
        mods.append((jnp.broadcast_to(m[0:1], (Bp, 6, D)), m[1:]))
    xp, xs = x_prompt, x_sample
    hp = hs = None
    states, keys_out, vals_out = [], [], []
    for i in range(DEPTH):
        j = i // 2
        mod_p, mod_s = mods[i]
        if i % 2 == 0:
            rw = (rwkv_mu[j], rwkv_wrkv[j], rwkv_wo[j], rwkv_w0[j], rwkv_w1[j], rwkv_w2[j],
                  rwkv_a0[j], rwkv_a1[j], rwkv_a2[j], rwkv_g1[j], rwkv_g2[j], rwkv_kk[j], rwkv_ka[j],
                  rwkv_rk[j], rwkv_lnx_g[j], rwkv_lnx_b[j])
            s_zero = jnp.zeros((2, Bp // SCAN_BATCH, RWKV_HEAD, RWKV_HEAD, LANES), F32)
            xp, hp, s_ctx = rwkv_sublayer(xp, mod_p, s_zero, ln_g[i, 0], ln_b[i, 0], *rw)
            xs, hs, _ = rwkv_sublayer(xs, mod_s, state_to_chains(state_rwkv[:, j]), ln_g[i, 0], ln_b[i, 0], *rw)
            states.append(state_from_chains(s_ctx))
        else:
            op, kp, vp = attention_context(hp, attn_wqkv[j], attn_wo[j], attn_qn[j], attn_kn[j])
            os_ = attention_latent(hs, cache_k[:, j], cache_v[:, j], attn_wqkv[j], attn_wo[j], attn_qn[j], attn_kn[j])
            keys_out.append(kp)
            vals_out.append(vp)
            xp, hp = ln_mod(xp, op, mod_p[:, 2:3], mod_p[:, 3:4], mod_p[:, 4:5], ln_g[i, 0], ln_b[i, 0])
            xs, hs = ln_mod(xs, os_, mod_s[:, 2:3], mod_s[:, 3:4], mod_s[:, 4:5], ln_g[i, 0], ln_b[i, 0])
        h_all = jnp.concatenate([hp.reshape(n_p, D), hs.reshape(-1, D)], axis=0)
        f_all = peer_ffn(h_all, peer_wq[i], peer_keys[i], peer_u[i].T.astype(BF16), peer_v[i].astype(BF16))
        fp = f_all[:n_p].reshape(xp.shape)
        fs = f_all[n_p:].reshape(xs.shape)
        nxt_p, nxt_s = mods[min(i + 1, DEPTH - 1)]
        xp, hp = ln_mod(xp, fp, mod_p[:, 5:6], nxt_p[:, 0:1], nxt_p[:, 1:2], ln_g[i, 1], ln_b[i, 1])
        xs, hs = ln_mod(xs, fs, mod_s[:, 5:6], nxt_s[:, 0:1], nxt_s[:, 1:2], ln_g[i, 1], ln_b[i, 1])
    return (xp, xs, jnp.stack(states, axis=1), jnp.stack(keys_out, axis=1), jnp.stack(vals_out, axis=1))
```

```python
import functools

import numpy as np
import jax
import jax.numpy as jnp
from jax import lax
from jax.experimental import pallas as pl
from jax.experimental.pallas import tpu as pltpu

D_MODEL = 1024
DEPTH = 4
GRID_W = 64
ALPHA = (2 * DEPTH) ** 0.25
LN_EPS = 1e-5
RWKV_HEAD = 64
RWKV_HEADS = D_MODEL // RWKV_HEAD
DECAY_LORA = 64
GN_EPS = RWKV_HEAD * 1e-5
HEAD_DIM = 64
N_HEADS = D_MODEL // HEAD_DIM
KV_HEADS = 4
GROUP = N_HEADS // KV_HEADS
Q_WIDTH = N_HEADS * HEAD_DIM
KV_WIDTH = KV_HEADS * HEAD_DIM
ROPE_THETA = 10000.0
ROPE_FREQS = HEAD_DIM // 4
ATTN_SCALE = HEAD_DIM ** -0.5
RMS_EPS = 1e-6
PEER_HEADS = 8
N_KEYS = 128
PEER_QUERY = 256
PEER_HALF = PEER_QUERY // 2
PEER_TOPK = 16

LANES = 128
MIB = 1024 * 1024
BF16 = jnp.bfloat16
F32 = jnp.float32


def _mm_kernel(x_ref, w_ref, o_ref):
    o_ref[...] = jnp.dot(x_ref[...].astype(BF16), w_ref[...].astype(BF16), preferred_element_type=F32)


def matmul(x, w, tm=512, tn=512):
    M, K = x.shape
    N = w.shape[1]
    tm = min(tm, M)
    tn = min(tn, N)
    assert M % tm == 0 and N % tn == 0, (M, N)
    return pl.pallas_call(
        _mm_kernel,
        grid=(M // tm, N // tn),
        in_specs=[pl.BlockSpec((tm, K), lambda i, j: (i, 0)),
                  pl.BlockSpec((K, tn), lambda i, j: (0, j))],
        out_specs=pl.BlockSpec((tm, tn), lambda i, j: (i, j)),
        out_shape=jax.ShapeDtypeStruct((M, N), F32),
        compiler_params=pltpu.CompilerParams(dimension_semantics=("parallel", "parallel")),
        name="matmul",
    )(x, w)


def mm3(x, w):
    return matmul(x.reshape(-1, x.shape[-1]), w).reshape(x.shape[:-1] + (w.shape[1],))


def _bdot(a, b):
    return jnp.dot(a.astype(BF16), b, preferred_element_type=F32)


SEG_BLOCK = 256


def _seg_ones():
    i = np.arange(SEG_BLOCK) // RWKV_HEAD
    return jnp.asarray((i[:, None] == i[None, :]).astype(np.float32), dtype=BF16)


def _segsum(q, seg):
    hi = q.astype(BF16)
    lo = (q - hi.astype(F32)).astype(BF16)
    return jnp.concatenate(
        [jnp.dot(hi[:, b:b + SEG_BLOCK], seg, preferred_element_type=F32)
         + jnp.dot(lo[:, b:b + SEG_BLOCK], seg, preferred_element_type=F32)
         for b in range(0, q.shape[1], SEG_BLOCK)], axis=1)


HEAD_PAIRS = D_MODEL // LANES
ROW_PAD = 8


def _store_head_rows(out_ref, val, slab_ref, tm):
    pitch = tm + ROW_PAD
    for p in range(HEAD_PAIRS):
        slab_ref[pl.ds(p * pitch, tm), :] = val[:, p * LANES:(p + 1) * LANES]

    def body(t, carry):
        out_ref[pl.ds(pl.multiple_of(t * HEAD_PAIRS, HEAD_PAIRS), HEAD_PAIRS), :] = (
            slab_ref[pl.ds(t, HEAD_PAIRS, stride=pitch), :])
        return carry

    lax.fori_loop(0, tm, body, 0, unroll=8)


def _load_head_rows(row_fn, slab_ref, tm):
    pitch = tm + ROW_PAD

    def body(t, carry):
        slab_ref[pl.ds(t, HEAD_PAIRS, stride=pitch), :] = row_fn(t)
        return carry

    lax.fori_loop(0, tm, body, 0, unroll=8)
    return jnp.concatenate([slab_ref[pl.ds(p * pitch, tm), :] for p in range(HEAD_PAIRS)], axis=1)


def _layer_norm_mod(z, g, b, shift, scale):
    mu = jnp.mean(z, -1, keepdims=True)
    zc = z - mu
    var = jnp.mean(zc * zc, -1, keepdims=True)
    xn = zc * lax.rsqrt(var + LN_EPS) * g + b
    return xn, xn * (1.0 + scale) + shift


def _ln_mod_kernel(x_ref, f_ref, gate_ref, shift_ref, scale_ref, vec_ref, xn_ref, h_ref):
    z = ALPHA * x_ref[...] + gate_ref[...] * f_ref[...]
    xn, h = _layer_norm_mod(z, vec_ref[0:1, :], vec_ref[1:2, :], shift_ref[...], scale_ref[...])
    xn_ref[...] = xn
    h_ref[...] = h


def ln_mod(x, f, gate, shift, scale, ln_g, ln_b, *, tm=256):
    B, T, D = x.shape
    vec = jnp.concatenate([ln_g[None], ln_b[None], jnp.zeros((6, D), F32)], axis=0)
    tok = pl.BlockSpec((None, tm, D), lambda b, i: (b, i, 0))
    per_b = pl.BlockSpec((None, 1, D), lambda b, i: (b, 0, 0))
    o1 = jax.ShapeDtypeStruct((B, T, D), F32)
    return pl.pallas_call(
        _ln_mod_kernel,
        grid=(B, T // tm),
        in_specs=[tok, tok, per_b, per_b, per_b, pl.BlockSpec((8, D), lambda b, i: (0, 0))],
        out_specs=[tok, tok],
        out_shape=[o1, o1],
        compiler_params=pltpu.CompilerParams(dimension_semantics=("parallel", "parallel")),
        name="ln_mod",
    )(x, f, gate, shift, scale, vec)


def _softplus(x):
    return jnp.maximum(x, 0.0) + jnp.log1p(jnp.exp(-jnp.abs(x)))


def _rwkv_pre_kernel(x_ref, xp_ref, xn_ref, shift_ref, scale_ref, mu_ref, vec_ref, wr_ref, wk_ref, wv_ref,
                     l1_ref, w2_ref, a2_ref, g2_ref, seg_ref,
                     r_ref, v_ref, kk_ref, w_ref, kd_ref, ka_ref, g_ref, bonus_ref, slab_ref, *, tm):
    i = pl.program_id(1)
    last = pl.num_programs(1) - 1
    one_scale = 1.0 + scale_ref[...]
    shift = shift_ref[...]
    h = x_ref[...] * one_scale + shift
    h_before = jnp.where(i == 0, 0.0, xp_ref[7:8, :] * one_scale + shift)
    h_after = jnp.where(i == last, 0.0, xn_ref[0:1, :] * one_scale + shift)
    row = lax.broadcasted_iota(jnp.int32, (tm, 1), 0)
    prev = jnp.where(row == 0, h_before, pltpu.roll(h, 1, axis=0))
    nxt = jnp.where(row == tm - 1, h_after, pltpu.roll(h, tm - 1, axis=0))
    xx = 0.5 * (prev + nxt) - h
    mix = lambda m: (h + xx * mu_ref[m:m + 1, :]).astype(BF16)
    w0, a0 = vec_ref[0:2, :], vec_ref[2:4, :]
    k_k, k_a, r_k = vec_ref[4:5, :], vec_ref[5:6, :], vec_ref[6:7, :]
    seg = seg_ref[...]
    r = jnp.dot(mix(0), wr_ref[...], preferred_element_type=F32)
    k = jnp.dot(mix(2), wk_ref[...], preferred_element_type=F32)
    v = jnp.dot(mix(3), wv_ref[...], preferred_element_type=F32)
    lw = jnp.tanh(jnp.dot(mix(1), l1_ref[0], preferred_element_type=F32))
    la = jnp.dot(mix(4), l1_ref[1], preferred_element_type=F32)
    lg = jax.nn.sigmoid(jnp.dot(mix(5), l1_ref[2], preferred_element_type=F32))
    g_ref[...] = _bdot(lg, g2_ref[...])
    kkr = k * k_k
    kk = kkr / jnp.maximum(jnp.sqrt(_segsum(kkr * kkr, seg)), 1e-12)
    _store_head_rows(r_ref, r, slab_ref, tm)
    _store_head_rows(v_ref, v, slab_ref, tm)
    _store_head_rows(kk_ref, kk, slab_ref, tm)
    kd_sum = jnp.zeros_like(k)
    for z in range(2):
        half = slice(z * DECAY_LORA, (z + 1) * DECAY_LORA)
        wl = w0[z:z + 1, :] + _bdot(lw[:, half], w2_ref[z])
        _store_head_rows(w_ref.at[z], jnp.exp(-jnp.exp(-_softplus(-wl) - 0.5)), slab_ref, tm)
        a = jax.nn.sigmoid(a0[z:z + 1, :] + _bdot(la[:, half], a2_ref[z]))
        kd = k * (1.0 + (a - 1.0) * k_a)
        _store_head_rows(kd_ref.at[z], kd, slab_ref, tm)
        _store_head_rows(ka_ref.at[z], kk * a, slab_ref, tm)
        kd_sum = kd_sum + r * kd * r_k
    bonus_ref[...] = _segsum(kd_sum, seg) * v


def rwkv_pre(x, shift, scale, mu, wrkv, w0, w1, w2, a0, a1, a2, g1, g2, k_k, k_a, r_k, *, tm=128):
    B, T, D = x.shape
    nb = tm // 8
    vec = jnp.concatenate([w0, a0, k_k[None], k_a[None], r_k.reshape(1, D), jnp.zeros((1, D), F32)], axis=0)
    l1 = jnp.stack([jnp.concatenate([w1[0], w1[1]], axis=1), jnp.concatenate([a1[0], a1[1]], axis=1), g1]).astype(BF16)
    tok = pl.BlockSpec((None, tm, D), lambda b, i: (b, i, 0))
    tok2 = pl.BlockSpec((2, None, tm, D), lambda b, i: (0, b, i, 0))
    whole = lambda a: pl.BlockSpec(a.shape, lambda b, i: (0,) * a.ndim)
    per_b = pl.BlockSpec((None, 1, D), lambda b, i: (b, 0, 0))
    consts = [mu, vec, wrkv[0].astype(BF16), wrkv[1].astype(BF16), wrkv[2].astype(BF16), l1,
              w2.astype(BF16), a2.astype(BF16), g2.astype(BF16), _seg_ones()]
    o1 = jax.ShapeDtypeStruct((B, T, D), F32)
    r1 = jax.ShapeDtypeStruct((B, T * HEAD_PAIRS, LANES), F32)
    r2 = jax.ShapeDtypeStruct((2, B, T * HEAD_PAIRS, LANES), F32)
    rows = pl.BlockSpec((None, tm * HEAD_PAIRS, LANES), lambda b, i: (b, i, 0))
    rows2 = pl.BlockSpec((2, None, tm * HEAD_PAIRS, LANES), lambda b, i: (0, b, i, 0))
    return pl.pallas_call(
        functools.partial(_rwkv_pre_kernel, tm=tm),
        grid=(B, T // tm),
        in_specs=[tok,
                  pl.BlockSpec((None, 8, D), lambda b, i: (b, jnp.maximum(i * nb - 1, 0), 0)),
                  pl.BlockSpec((None, 8, D), lambda b, i: (b, jnp.minimum((i + 1) * nb, T // 8 - 1), 0)),
                  per_b, per_b] + [whole(a) for a in consts],
        out_specs=[rows, rows, rows, rows2, rows2, rows2, tok, tok],
        out_shape=[r1, r1, r1, r2, r2, r2, o1, o1],
        scratch_shapes=[pltpu.VMEM((HEAD_PAIRS * (tm + ROW_PAD), LANES), F32)],
        compiler_params=pltpu.CompilerParams(dimension_semantics=("parallel", "parallel"),
                                             vmem_limit_bytes=48 * MIB),
        name="rwkv_pre",
    )(x, x, x, shift, scale, *consts)


def _rwkv_post_kernel(y_ref, bonus_ref, g_ref, x_ref, gate_ref, shift_ref, scale_ref, vec_ref, wo_ref, seg_ref,
                      xn_ref, h_ref, slab_ref, *, tm):
    seg = seg_ref[...]

    def y_rows(t):
        rows = pl.ds(pl.multiple_of(t * HEAD_PAIRS, HEAD_PAIRS), HEAD_PAIRS)
        return y_ref[0, rows, :] + y_ref[1, rows, :]

    y = _load_head_rows(y_rows, slab_ref, tm)
    yc = y - _segsum(y, seg) * (1.0 / RWKV_HEAD)
    yv = _segsum(yc * yc, seg) * (1.0 / RWKV_HEAD)
    yn = yc * lax.rsqrt(yv + GN_EPS) * vec_ref[0:1, :] + vec_ref[1:2, :]
    o = _bdot((yn + bonus_ref[...]) * g_ref[...], wo_ref[...])
    z = ALPHA * x_ref[...] + gate_ref[...] * o
    xn, h = _layer_norm_mod(z, vec_ref[2:3, :], vec_ref[3:4, :], shift_ref[...], scale_ref[...])
    xn_ref[...] = xn
    h_ref[...] = h


def rwkv_post(y, bonus, g, x, gate, shift, scale, lnx_g, lnx_b, wo, ln_g, ln_b, *, tm=256):
    B, T, D = x.shape
    vec = jnp.concatenate([lnx_g[None], lnx_b[None], ln_g[None], ln_b[None], jnp.zeros((4, D), F32)], axis=0)
    tok = pl.BlockSpec((None, tm, D), lambda b, i: (b, i, 0))
    per_b = pl.BlockSpec((None, 1, D), lambda b, i: (b, 0, 0))
    whole = lambda a: pl.BlockSpec(a.shape, lambda b, i: (0,) * a.ndim)
    consts = [vec, wo.astype(BF16), _seg_ones()]
    o1 = jax.ShapeDtypeStruct((B, T, D), F32)
    return pl.pallas_call(
        functools.partial(_rwkv_post_kernel, tm=tm),
        grid=(B, T // tm),
        in_specs=[pl.BlockSpec((2, None, tm * HEAD_PAIRS, LANES), lambda b, i: (0, b, i, 0)),
                  tok, tok, tok, per_b, per_b, per_b] + [whole(a) for a in consts],
        out_specs=[tok, tok],
        out_shape=[o1, o1],
        scratch_shapes=[pltpu.VMEM((HEAD_PAIRS * (tm + ROW_PAD), LANES), F32)],
        compiler_params=pltpu.CompilerParams(dimension_semantics=("parallel", "parallel")),
        name="rwkv_post",
    )(y, bonus, g, x, gate, shift, scale, *consts)


SCAN_ROWS = 32
SCAN_CHAINS = 4
SCAN_BATCH = 8


def _tree_sum(parts):
    while len(parts) > 1:
        parts = [parts[i] + parts[i + 1] for i in range(0, len(parts), 2)]
    return parts[0]


def _to_chain_tiles(src_ref, t0):
    z = jnp.concatenate([src_ref[:, pl.ds((t0 + s) * HEAD_PAIRS, HEAD_PAIRS), :].reshape(SCAN_BATCH * HEAD_PAIRS, LANES)
                         for s in range(2)], axis=0)
    zt = z.T
    top, bot = zt[:RWKV_HEAD], zt[RWKV_HEAD:]
    low = lax.broadcasted_iota(jnp.int32, (RWKV_HEAD, LANES), 1) < RWKV_HEAD
    return (jnp.where(low, top, pltpu.roll(bot, RWKV_HEAD, axis=1)),
            jnp.where(low, pltpu.roll(top, RWKV_HEAD, axis=1), bot))


def _from_chain_tiles(y0, y1):
    low = lax.broadcasted_iota(jnp.int32, (RWKV_HEAD, LANES), 1) < RWKV_HEAD
    top = jnp.where(low, y0, pltpu.roll(y1, RWKV_HEAD, axis=1))
    bot = jnp.where(low, pltpu.roll(y0, RWKV_HEAD, axis=1), y1)
    return jnp.concatenate([top, bot], axis=0).T


def _wkv_scan_kernel(r_ref, w_ref, k_ref, v_ref, kk_ref, ka_ref, s0_ref, y_ref, sf_ref,
                     st_ref, rs, ws, ks_, vs, kks, kas, *, tc):
    d = pl.program_id(0)
    c = pl.program_id(2)
    n = RWKV_HEAD
    npairs = tc // 2
    srcs = (r_ref, w_ref, k_ref, v_ref, kk_ref, ka_ref)
    dsts = (rs, ws, ks_, vs, kks, kas)
    fwd = d == 0

    @pl.when(c == 0)
    def _init():
        st_ref[...] = s0_ref[...]

    def pair_start(p):
        return 2 * jnp.where(fwd, p, npairs - 1 - p)

    def stage(p, slot):
        t0 = pair_start(jnp.minimum(p, npairs - 1))
        for src, dst in zip(srcs, dsts):
            a, b = _to_chain_tiles(src, t0)
            dst[2 * slot] = jnp.where(fwd, a, b)
            dst[2 * slot + 1] = jnp.where(fwd, b, a)

    def one_step(i):
        halves = []
        for r0 in range(0, n, SCAN_ROWS):
            rows = pl.ds(r0, SCAN_ROWS)
            parts = [st_ref[j, rows, :] * kks[i, pl.ds(j, 1), :] for j in range(SCAN_CHAINS)]
            for j in range(SCAN_CHAINS, n):
                parts[j % SCAN_CHAINS] = parts[j % SCAN_CHAINS] + st_ref[j, rows, :] * kks[i, pl.ds(j, 1), :]
            skk = _tree_sum(parts)
            v_t = vs[i, rows, :]
            parts = []
            for j in range(n):
                s = (st_ref[j, rows, :] * ws[i, pl.ds(j, 1), :] - skk * kas[i, pl.ds(j, 1), :]
                     + v_t * ks_[i, pl.ds(j, 1), :])
                st_ref[j, rows, :] = s
                p = s * rs[i, pl.ds(j, 1), :]
                if j < SCAN_CHAINS:
                    parts.append(p)
                else:
                    parts[j % SCAN_CHAINS] = parts[j % SCAN_CHAINS] + p
            halves.append(_tree_sum(parts))
        return jnp.concatenate(halves, axis=0)

    def run_pair(p, slot):
        stage(p + 1, 1 - slot)
        y_a = one_step(2 * slot)
        y_b = one_step(2 * slot + 1)
        out = _from_chain_tiles(jnp.where(fwd, y_a, y_b), jnp.where(fwd, y_b, y_a))
        t0 = pair_start(p)
        for s in range(2):
            y_ref[:, pl.ds((t0 + s) * HEAD_PAIRS, HEAD_PAIRS), :] = out[s * 64:(s + 1) * 64].reshape(
                SCAN_BATCH, HEAD_PAIRS, LANES)

    stage(0, 0)

    def two_pairs(q, carry):
        run_pair(2 * q, 0)
        run_pair(2 * q + 1, 1)
        return carry

    lax.fori_loop(0, npairs // 2, two_pairs, 0)

    @pl.when(c == pl.num_programs(2) - 1)
    def _fin():
        sf_ref[...] = st_ref[...]


def wkv_scan_tokens(r, w, k, v, kk, ka, s0, *, tc=32):
    B = r.shape[0]
    T = r.shape[1] // HEAD_PAIRS
    n = RWKV_HEAD
    G = B // SCAN_BATCH
    assert r.shape[2] == LANES and B % SCAN_BATCH == 0 and T % tc == 0 and tc % 4 == 0
    nc = T // tc
    cm = lambda d, c: jnp.where(d == 0, c, nc - 1 - c)
    shared = pl.BlockSpec((SCAN_BATCH, tc * HEAD_PAIRS, LANES), lambda d, g, c: (g, cm(d, c), 0))
    perdir = pl.BlockSpec((None, SCAN_BATCH, tc * HEAD_PAIRS, LANES), lambda d, g, c: (d, g, cm(d, c), 0))
    state = pl.BlockSpec((None, None, n, n, LANES), lambda d, g, c: (d, g, 0, 0, 0))
    tile = pltpu.VMEM((4, n, LANES), F32)
    return pl.pallas_call(
        functools.partial(_wkv_scan_kernel, tc=tc),
        grid=(2, G, nc),
        in_specs=[shared, perdir, perdir, shared, shared, perdir, state],
        out_specs=[perdir, state],
        out_shape=[jax.ShapeDtypeStruct((2, B, T * HEAD_PAIRS, LANES), F32),
                   jax.ShapeDtypeStruct((2, G, n, n, LANES), F32)],
        scratch_shapes=[pltpu.VMEM((n, n, LANES), F32)] + [tile] * 6,
        compiler_params=pltpu.CompilerParams(dimension_semantics=("parallel", "parallel", "arbitrary")),
        name="wkv_scan",
    )(r, w, k, v, kk, ka, s0)


def state_to_chains(s):
    B = s.shape[0]
    s = s.reshape(B // SCAN_BATCH, SCAN_BATCH, 2, HEAD_PAIRS, 2, RWKV_HEAD, RWKV_HEAD)
    return s.transpose(2, 0, 6, 5, 4, 1, 3).reshape(2, B // SCAN_BATCH, RWKV_HEAD, RWKV_HEAD, LANES)


def state_from_chains(s):
    G = s.shape[1]
    s = s.reshape(2, G, RWKV_HEAD, RWKV_HEAD, 2, SCAN_BATCH, HEAD_PAIRS)
    return s.transpose(1, 5, 0, 6, 4, 3, 2).reshape(G * SCAN_BATCH, 2, 2 * HEAD_PAIRS, RWKV_HEAD, RWKV_HEAD)


def rwkv_sublayer(x, mod, s0_chains, ln_g, ln_b, mu, wrkv, wo, w0, w1, w2, a0, a1, a2, g1, g2, k_k, k_a, r_k,
                  lnx_g, lnx_b):
    r, v, kk, w, kd, ka, g, bonus = rwkv_pre(x, mod[:, 0:1], mod[:, 1:2], mu, wrkv, w0, w1, w2, a0, a1, a2,
                                             g1, g2, k_k, k_a, r_k)
    y, s_fin = wkv_scan_tokens(r, w, kd, v, kk, ka, s0_chains)
    xn, h = rwkv_post(y, bonus, g, x, mod[:, 2:3], mod[:, 3:4], mod[:, 4:5], lnx_g, lnx_b, wo, ln_g, ln_b)
    return xn, h, s_fin


def rms_norm(x, g):
    return x * lax.rsqrt(jnp.mean(x * x, -1, keepdims=True) + RMS_EPS) * g


def axial_rope_angles(n):
    rows = n // GRID_W
    row = jnp.repeat(jnp.arange(rows), GRID_W).astype(F32)
    col = (jnp.arange(rows * GRID_W) % GRID_W).astype(F32)
    freqs = ROPE_THETA ** (-jnp.arange(ROPE_FREQS, dtype=F32) / ROPE_FREQS)
    ang = jnp.stack([row[:, None] * freqs, col[:, None] * freqs], axis=1)
    return jnp.cos(ang), jnp.sin(ang)


def apply_axial_rope(x, cos, sin):
    xs = x.reshape(x.shape[:-1] + (2, 2, ROPE_FREQS))
    bshape = (x.shape[1],) + (1,) * (x.ndim - 3) + (2, ROPE_FREQS)
    cos, sin = cos.reshape(bshape), sin.reshape(bshape)
    x1, x2 = xs[..., 0, :], xs[..., 1, :]
    out = jnp.stack([x1 * cos - x2 * sin, x2 * cos + x1 * sin], axis=-2)
    return out.reshape(x.shape)


def attn_project(h, wqkv, qn, kn):
    B, n, _ = h.shape
    qkv = mm3(h, wqkv)
    q = rms_norm(qkv[..., :Q_WIDTH].reshape(B, n, KV_HEADS, GROUP, HEAD_DIM), qn)
    k = rms_norm(qkv[..., Q_WIDTH:Q_WIDTH + KV_WIDTH].reshape(B, n, KV_HEADS, HEAD_DIM), kn)
    v = qkv[..., Q_WIDTH + KV_WIDTH:].reshape(B, n, KV_HEADS, HEAD_DIM)
    return q, k, v


def _attn_kernel(q_ref, kt_ref, v_ref, o_ref):
    kt = kt_ref[...].astype(BF16)
    v = v_ref[...].astype(BF16)
    outs = []
    for g in range(GROUP):
        qg = q_ref[:, g * HEAD_DIM:(g + 1) * HEAD_DIM].astype(BF16)
        s = jnp.dot(qg, kt, preferred_element_type=F32) * ATTN_SCALE
        e = jnp.exp(s - jnp.max(s, axis=-1, keepdims=True))
        l = jnp.sum(e, axis=-1, keepdims=True)
        outs.append(jnp.dot(e.astype(BF16), v, preferred_element_type=F32) / l)
    o_ref[...] = jnp.concatenate(outs, axis=-1)


def attention(q, k, v, *, tq=256):
    B, n = q.shape[:2]
    S = k.shape[1]
    gw = GROUP * HEAD_DIM
    return pl.pallas_call(
        _attn_kernel,
        grid=(B, KV_HEADS, n // tq),
        in_specs=[pl.BlockSpec((None, tq, gw), lambda b, h, i: (b, i, h)),
                  pl.BlockSpec((None, None, HEAD_DIM, S), lambda b, h, i: (b, h, 0, 0)),
                  pl.BlockSpec((None, None, S, HEAD_DIM), lambda b, h, i: (b, h, 0, 0))],
        out_specs=pl.BlockSpec((None, tq, gw), lambda b, h, i: (b, i, h)),
        out_shape=jax.ShapeDtypeStruct((B, n, Q_WIDTH), F32),
        compiler_params=pltpu.CompilerParams(dimension_semantics=("parallel", "parallel", "parallel")),
        name="attention",
    )(q.reshape(B, n, Q_WIDTH), k.transpose(0, 2, 3, 1), v.transpose(0, 2, 1, 3))


def attention_context(h, wqkv, wo, qn, kn):
    q, k, v = attn_project(h, wqkv, qn, kn)
    return mm3(attention(q, k, v), wo), k, v


def attention_latent(h, ctx_k, ctx_v, wqkv, wo, qn, kn):
    q, k, v = attn_project(h, wqkv, qn, kn)
    cos, sin = axial_rope_angles(h.shape[1])
    q = apply_axial_rope(q, cos, sin)
    k = apply_axial_rope(k, cos, sin)
    keys = jnp.concatenate([ctx_k, k], axis=1)
    vals = jnp.concatenate([ctx_v, v], axis=1)
    return mm3(attention(q, keys, vals), wo)


CUBE_PITCH = 136


def _gelu(x):
    return 0.5 * x * (1.0 + lax.erf(x * (2.0 ** -0.5)))


def _gate_tile(a_row, b_row, g_row, row_id):
    at = jnp.where(row_id == a_row, g_row, 0.0).astype(BF16)
    bt = jnp.where(row_id == b_row, 1.0, 0.0).astype(BF16)
    return lax.dot_general(at, bt, (((1,), (1,)), ((), ())), preferred_element_type=F32)


def _peer_experts_kernel(x_ref, a_ref, b_ref, g_ref, ut_ref, v_ref, o_ref, xb_ref, cube_ref, *, tm, ib):
    e = pl.program_id(1)

    @pl.when(e == 0)
    def _build():
        xb_ref[...] = x_ref[...].astype(BF16)
        o_ref[...] = jnp.zeros_like(o_ref)
        row_id = lax.broadcasted_iota(jnp.int32, (N_KEYS, N_KEYS), 0)

        def per_token(t, carry):
            cube_ref[pl.ds(pl.multiple_of(t * CUBE_PITCH, 8), N_KEYS), :] = _gate_tile(
                a_ref[pl.ds(t, 1), :], b_ref[pl.ds(t, 1), :], g_ref[pl.ds(t, 1), :], row_id)
            return carry

        lax.fori_loop(0, tm, per_token, 0, unroll=16)

    s = jnp.dot(xb_ref[...], ut_ref[...], preferred_element_type=F32)
    gates = jnp.concatenate([cube_ref[pl.ds(e * ib + p, tm, stride=CUBE_PITCH), :] for p in range(ib)], axis=1)
    h = (gates * _gelu(s)).astype(BF16)
    o_ref[...] += jnp.dot(h, v_ref[...], preferred_element_type=F32)


def peer_experts(x, a_idx, b_idx, gate, ut_bf16, v_bf16, *, tm=256, ib=16):
    T, D = x.shape
    P = a_idx.shape[1]
    assert P == N_KEYS and T % tm == 0 and N_KEYS % ib == 0
    kern = functools.partial(_peer_experts_kernel, tm=tm, ib=ib)
    return pl.pallas_call(
        kern,
        grid=(T // tm, N_KEYS // ib),
        in_specs=[pl.BlockSpec((tm, D), lambda i, e: (i, 0)),
                  pl.BlockSpec((tm, P), lambda i, e: (i, 0)),
                  pl.BlockSpec((tm, P), lambda i, e: (i, 0)),
                  pl.BlockSpec((tm, P), lambda i, e: (i, 0)),
                  pl.BlockSpec((D, ib * N_KEYS), lambda i, e: (0, e)),
                  pl.BlockSpec((ib * N_KEYS, D), lambda i, e: (e, 0))],
        out_specs=pl.BlockSpec((tm, D), lambda i, e: (i, 0)),
        out_shape=jax.ShapeDtypeStruct((T, D), F32),
        scratch_shapes=[pltpu.VMEM((tm, D), BF16),
                        pltpu.VMEM((tm * CUBE_PITCH, N_KEYS), F32)],
        compiler_params=pltpu.CompilerParams(dimension_semantics=("parallel", "arbitrary"),
                                             vmem_limit_bytes=52 * MIB),
        name="peer_experts",
    )(x, a_idx, b_idx, gate, ut_bf16, v_bf16)


_CAND_COLS = (16, 8, 8, 8, 8, 8, 8, 8)
_CAND_ROWS = 80


def _cand_tables():
    idx = np.zeros((_CAND_ROWS, 1), np.int32)
    valid = np.zeros((_CAND_ROWS, 1), np.float32)
    pos = 0
    for r, n in enumerate(_CAND_COLS):
        for cc in range(n):
            idx[pos] = r * PEER_TOPK + cc
            valid[pos] = 1.0 if (r + 1) * (cc + 1) <= PEER_TOPK else 0.0
            pos += 1
    for r in range(8, 16):
        idx[pos] = r * PEER_TOPK
        valid[pos] = 1.0
        pos += 1
    assert pos == _CAND_ROWS
    return (jnp.asarray(np.broadcast_to(idx, (_CAND_ROWS, LANES)).copy()),
            jnp.asarray(np.broadcast_to(valid, (_CAND_ROWS, LANES)).copy()))


def _top_rows(s, ids, k, sentinel):
    vals, picks = [], []
    for _ in range(k):
        m = jnp.max(s, axis=0, keepdims=True)
        pick = jnp.min(jnp.where(s == m, ids, sentinel), axis=0, keepdims=True)
        vals.append(m)
        picks.append(pick)
        s = jnp.where(ids == pick, -jnp.inf, s)
    return jnp.concatenate(vals, axis=0), jnp.concatenate(picks, axis=0)


def _take_rows(table, rows):
    out = jnp.zeros(rows.shape, table.dtype)
    for rr in range(table.shape[0]):
        out = jnp.where(rows == rr, table[rr:rr + 1, :], out)
    return out


def _peer_route_kernel(x_ref, wqt_ref, keys_ref, cidx_ref, cvalid_ref, a_ref, b_ref, g_ref, q_ref, *, tm):
    xb = x_ref[...].astype(BF16)
    q_ref[...] = lax.dot_general(wqt_ref[...], xb, (((1,), (1,)), ((), ())),
                                 preferred_element_type=F32).astype(BF16)
    key_ids = lax.broadcasted_iota(jnp.int32, (N_KEYS, LANES), 0)

    def where(it):
        return it % PEER_HEADS, pl.multiple_of((it // PEER_HEADS) * LANES, LANES)

    def sub_key_tops(it):
        h, lane0 = where(it)
        out = []
        for z in range(2):
            row0 = pl.multiple_of((h * 2 + z) * PEER_HALF, PEER_HALF)
            qhz = q_ref[pl.ds(row0, PEER_HALF), pl.ds(lane0, LANES)]
            s = jnp.dot(keys_ref[z], qhz, preferred_element_type=F32)
            out.extend(_top_rows(s, key_ids, PEER_TOPK, N_KEYS))
        return tuple(out)

    def pick_experts(it, tops):
        h, lane0 = where(it)
        sv0, si0, sv1, si1 = tops
        pieces = [sv0[r:r + 1, :] + sv1[:n, :] for r, n in enumerate(_CAND_COLS)]
        pieces.append(sv0[8:16, :] + sv1[0:1, :])
        cand = jnp.where(cvalid_ref[...] > 0.0, jnp.concatenate(pieces, axis=0), -jnp.inf)
        cv, ci = _top_rows(cand, cidx_ref[...], PEER_TOPK, PEER_TOPK * PEER_TOPK)
        i1 = _take_rows(si0, ci >> 4)
        i2 = _take_rows(si1, ci & (PEER_TOPK - 1))
        e = jnp.exp(cv - jnp.max(cv, axis=0, keepdims=True))
        gate = e / jnp.sum(e, axis=0, keepdims=True)
        rows = pl.ds(pl.multiple_of(h * PEER_TOPK, PEER_TOPK), PEER_TOPK)
        a_ref[rows, pl.ds(lane0, LANES)] = i1
        b_ref[rows, pl.ds(lane0, LANES)] = i2
        g_ref[rows, pl.ds(lane0, LANES)] = gate

    def body(it, tops):
        pick_experts(it - 1, tops)
        return sub_key_tops(it)

    n_items = PEER_HEADS * (tm // LANES)
    pick_experts(n_items - 1, lax.fori_loop(1, n_items, body, sub_key_tops(0)))


def peer_route(x, wqt_bf16, keys_bf16, *, tm=256):
    T, D = x.shape
    QW = wqt_bf16.shape[0]
    P = PEER_HEADS * PEER_TOPK
    cidx, cvalid = _cand_tables()
    whole = lambda shape: pl.BlockSpec(shape, lambda i: (0,) * len(shape))
    out = pl.BlockSpec((P, tm), lambda i: (0, i))
    return pl.pallas_call(
        functools.partial(_peer_route_kernel, tm=tm),
        grid=(T // tm,),
        in_specs=[pl.BlockSpec((tm, D), lambda i: (i, 0)), whole((QW, D)), whole((2, N_KEYS, PEER_HALF)),
                  whole((_CAND_ROWS, LANES)), whole((_CAND_ROWS, LANES))],
        out_specs=[out, out, out],
        out_shape=[jax.ShapeDtypeStruct((P, T), jnp.int32), jax.ShapeDtypeStruct((P, T), jnp.int32),
                   jax.ShapeDtypeStruct((P, T), F32)],
        scratch_shapes=[pltpu.VMEM((QW, tm), BF16)],
        compiler_params=pltpu.CompilerParams(dimension_semantics=("parallel",)),
        name="peer_route",
    )(x, wqt_bf16, keys_bf16, cidx, cvalid)


def peer_ffn(x, wq, sub_keys, ut_bf16, v_bf16):
    a, b, g = peer_route(x, wq.T.astype(BF16), sub_keys.astype(BF16))
    return peer_experts(x, a.T, b.T, g.T, ut_bf16, v_bf16)


def kernel(x_prompt, x_sample, c, state_rwkv, cache_k, cache_v, c_ctx, ada_w, ada_b, ln_g, ln_b,
           rwkv_mu, rwkv_wrkv, rwkv_wo, rwkv_w0, rwkv_w1, rwkv_w2, rwkv_a0, rwkv_a1, rwkv_a2,
           rwkv_g1, rwkv_g2, rwkv_kk, rwkv_ka, rwkv_rk, rwkv_lnx_g, rwkv_lnx_b,
           attn_wqkv, attn_wo, attn_qn, attn_kn, peer_wq, peer_keys, peer_u, peer_v):
    Bp, Tp, D = x_prompt.shape
    Bs, Ts, _ = x_sample.shape
    n_p = Bp * Tp
    cond = jax.nn.silu(jnp.concatenate([c_ctx[None], c], axis=0))
    mods = []
    for i in range(DEPTH):
        m = (matmul(cond, ada_w[i]) + ada_b[i]).reshape(1 + Bs, 6, D)
        mods.append((jnp.broadcast_to(m[0:1], (Bp, 6, D)), m[1:]))
    xp, xs = x_prompt, x_sample
    hp = hs = None
    states, keys_out, vals_out = [], [], []
    for i in range(DEPTH):
        j = i // 2
        mod_p, mod_s = mods[i]
        if i % 2 == 0:
            rw = (rwkv_mu[j], rwkv_wrkv[j], rwkv_wo[j], rwkv_w0[j], rwkv_w1[j], rwkv_w2[j],
                  rwkv_a0[j], rwkv_a1[j], rwkv_a2[j], rwkv_g1[j], rwkv_g2[j], rwkv_kk[j], rwkv_ka[j],
                  rwkv_rk[j], rwkv_lnx_g[j], rwkv_lnx_b[j])
            s_zero = jnp.zeros((2, Bp // SCAN_BATCH, RWKV_HEAD, RWKV_HEAD, LANES), F32)
            xp, hp, s_ctx = rwkv_sublayer(xp, mod_p, s_zero, ln_g[i, 0], ln_b[i, 0], *rw)
            xs, hs, _ = rwkv_sublayer(xs, mod_s, state_to_chains(state_rwkv[:, j]), ln_g[i, 0], ln_b[i, 0], *rw)
            states.append(state_from_chains(s_ctx))
        else:
            op, kp, vp = attention_context(hp, attn_wqkv[j], attn_wo[j], attn_qn[j], attn_kn[j])
            os_ = attention_latent(hs, cache_k[:, j], cache_v[:, j], attn_wqkv[j], attn_wo[j], attn_qn[j], attn_kn[j])
            keys_out.append(kp)
            vals_out.append(vp)
            xp, hp = ln_mod(xp, op, mod_p[:, 2:3], mod_p[:, 3:4], mod_p[:, 4:5], ln_g[i, 0], ln_b[i, 0])
            xs, hs = ln_mod(xs, os_, mod_s[:, 2:3], mod_s[:, 3:4], mod_s[:, 4:5], ln_g[i, 0], ln_b[i, 0])
        ut_bf16, v_bf16 = peer_u[i].T.astype(BF16), peer_v[i].astype(BF16)
        fp = peer_ffn(hp.reshape(n_p, D), peer_wq[i], peer_keys[i], ut_bf16, v_bf16).reshape(xp.shape)
        fs = peer_ffn(hs.reshape(-1, D), peer_wq[i], peer_keys[i], ut_bf16, v_bf16).reshape(xs.shape)
        nxt_p, nxt_s = mods[min(i + 1, DEPTH - 1)]
        xp, hp = ln_mod(xp, fp, mod_p[:, 5:6], nxt_p[:, 0:1], nxt_p[:, 1:2], ln_g[i, 1], ln_b[i, 1])
        xs, hs = ln_mod(xs, fs, mod_s[:, 5:6], nxt_s[:, 0:1], nxt_s[:, 1:2], ln_g[i, 1], ln_b[i, 1])
    return (xp, xs, jnp.stack(states, axis=1), jnp.stack(keys_out, axis=1), jnp.stack(vals_out, axis=1))
```
